```python
import functools
import jax, jax.numpy as jnp
from jax import lax
import numpy as np

D_MODEL = 1024
BATCH = 16
SEQ = 2048
DEPTH = 4
DEC_BATCH = 128
DEC_SEQ = 1
PAST_LEN = 8192
PAGE_SIZE = 128

MLA_HEADS = 8
MLA_Q_LORA = 256
MLA_KV_LORA = 128
MLA_NOPE = 64
MLA_ROPE = 32
MLA_V = 64
ROPE_BASE = 10000.0
MLA_SCALE = (MLA_NOPE + MLA_ROPE) ** -0.5
SB_HEADS = 8
SB_KV_HEADS = 2
SB_GROUP = SB_HEADS // SB_KV_HEADS
SB_HD = 64
SB_SCALE = SB_HD ** -0.5
NSA_HEADS = 8
NSA_KV_HEADS = 1
NSA_GROUP = NSA_HEADS // NSA_KV_HEADS
NSA_HD = 64
NSA_SCALE = NSA_HD ** -0.5
CMP_BLOCK = 32
SEL_BLOCK = 64
CMP_PER_SEL = SEL_BLOCK // CMP_BLOCK
SEL_TOPN = 8
WINDOW = 512
FORCE_BONUS = 1.0e4
ALIBI_MAX_EXP = 8.0
D_FF = 2816
CONV_W = 3
QBLOCK = 128
NORM_EPS = 1e-6
NEG_INF = -1e30

IN_WIDTHS = (MLA_Q_LORA, MLA_KV_LORA, MLA_ROPE,
             SB_HEADS * SB_HD, SB_KV_HEADS * SB_HD, SB_KV_HEADS * SB_HD,
             NSA_HEADS * NSA_HD, 6 * NSA_KV_HEADS * NSA_HD, 3 * NSA_HEADS,
             3 * D_MODEL)
N_IN = sum(IN_WIDTHS)

kernel_name = "hybrid_mla_stickbreak_nsa_convffn_step"


def rmsnorm(x, g):
    xf = x.astype(jnp.float32)
    y = xf * lax.rsqrt(jnp.mean(xf * xf, axis=-1, keepdims=True) + NORM_EPS)
    return (y * g.astype(jnp.float32)).astype(x.dtype)


def rope(x, posf):
    half = x.shape[-1] // 2
    inv = ROPE_BASE ** (-jnp.arange(half, dtype=jnp.float32) / half)
    ang = posf[:, None] * inv[None, :]
    shape = (ang.shape[0],) + (1,) * (x.ndim - 3) + (half,)
    cos = jnp.cos(ang).reshape(shape)
    sin = jnp.sin(ang).reshape(shape)
    x1 = x[..., :half].astype(jnp.float32)
    x2 = x[..., half:].astype(jnp.float32)
    return jnp.concatenate([x1 * cos - x2 * sin, x2 * cos + x1 * sin], axis=-1).astype(x.dtype)


def alibi_slopes():
    h = jnp.arange(1, NSA_HEADS + 1, dtype=jnp.float32)
    return (2.0 ** (-ALIBI_MAX_EXP * h / NSA_HEADS)).reshape(NSA_KV_HEADS, NSA_GROUP)


def masked_softmax(s, mask):
    p = jax.nn.softmax(jnp.where(mask, s, NEG_INF), axis=-1)
    return jnp.where(mask, p, 0.0)


def sweep_queries(fn, q_pos, *q_args):
    n_q = q_pos.shape[0]
    qb = QBLOCK if n_q % QBLOCK == 0 else n_q
    nb = n_q // qb
    if nb == 1:
        return fn(q_pos, *q_args)
    def to_blocks(a):
        return jnp.moveaxis(a.reshape((a.shape[0], nb, qb) + a.shape[2:]), 1, 0)
    xs = (q_pos.reshape(nb, qb),) + tuple(to_blocks(a) for a in q_args)
    out = lax.map(lambda xb: fn(xb[0], *xb[1:]), xs)
    out = jnp.moveaxis(out, 0, 1)
    return out.reshape((out.shape[0], n_q) + out.shape[3:])


def gather_pages(pool, page_table):
    g = pool[page_table]
    return g.reshape((g.shape[0], g.shape[1] * g.shape[2]) + g.shape[3:])


def mla_attend(ckv, krope, q_pos, q_lat, q_rope):
    k_pos = jnp.arange(ckv.shape[1])
    s = (jnp.einsum('bqhr,bkr->bhqk', q_lat, ckv)
         + jnp.einsum('bqhp,bkp->bhqk', q_rope, krope)).astype(jnp.float32)
    p = masked_softmax(s, k_pos[None, :] <= q_pos[:, None])
    return jnp.einsum('bhqk,bkr->bqhr', p.astype(ckv.dtype), ckv)


def sb_attend(k_all, v_all, q_pos, q):
    k_pos = jnp.arange(k_all.shape[1])
    z = jnp.einsum('bqkgd,bskd->bkgqs', q, k_all).astype(jnp.float32) * SB_SCALE
    mask = k_pos[None, :] < q_pos[:, None]
    log_stay = jnp.where(mask, jax.nn.log_sigmoid(-z), 0.0)
    later = lax.cumsum(log_stay, axis=4, reverse=True) - log_stay
    a = jnp.where(mask, jnp.exp(jax.nn.log_sigmoid(z) + later), 0.0)
    return jnp.einsum('bkgqs,bskd->bqkgd', a.astype(v_all.dtype), v_all)


def pool_blocks(rows, w):
    n = rows.shape[1] // CMP_BLOCK
    r = rows[:, :n * CMP_BLOCK].reshape((rows.shape[0], n, CMP_BLOCK) + rows.shape[2:])
    return jnp.einsum('bnjkd,j->bnkd', r, w)


def to_sel_blocks(rows):
    tk = rows.shape[1]
    n_sel = -(-tk // SEL_BLOCK)
    r = jnp.pad(rows, ((0, 0), (0, n_sel * SEL_BLOCK - tk), (0, 0), (0, 0)))
    r = r.reshape((rows.shape[0], n_sel, SEL_BLOCK) + rows.shape[2:])
    return jnp.moveaxis(r, 3, 1)


def nsa_attend(kc, vc, sel_k, sel_v, win_k, win_v, win_pos0, slopes, q_pos, q):
    b, nq = q.shape[0], q.shape[1]
    qf = q_pos.astype(jnp.float32)
    n_cmp = kc.shape[1]
    cmp_end = (jnp.arange(n_cmp) + 1) * CMP_BLOCK - 1
    s = (jnp.einsum('bqkgd,bnkd->bkgqn', q, kc).astype(jnp.float32) * NSA_SCALE
         - slopes[:, :, None, None] * (qf[:, None] - cmp_end.astype(jnp.float32)[None, :]))
    p_c = masked_softmax(s, cmp_end[None, :] <= q_pos[:, None])
    o_c = jnp.einsum('bkgqn,bnkd->bqkgd', p_c.astype(vc.dtype), vc)
    n_sel = sel_k.shape[2]
    imp = jnp.sum(p_c, axis=2)
    imp = jnp.pad(imp, ((0, 0), (0, 0), (0, 0), (0, n_sel * CMP_PER_SEL - n_cmp)))
    imp = imp.reshape(imp.shape[:3] + (n_sel, CMP_PER_SEL)).sum(-1)
    blk = jnp.arange(n_sel)[None, :]
    cur = (q_pos // SEL_BLOCK)[:, None]
    forced = (blk == 0) | (blk == cur) | (blk == cur - 1)
    valid = blk * SEL_BLOCK <= q_pos[:, None]
    score = jnp.where(valid, imp + jnp.where(forced, FORCE_BONUS, 0.0), -1.0)
    top_n = min(SEL_TOPN, n_sel)
    top_val, top_idx = lax.top_k(score, top_n)
    chosen = top_val >= 0.0
    b_ix = jnp.arange(b)[:, None, None, None]
    k_ix = jnp.arange(NSA_KV_HEADS)[None, :, None, None]
    gk = sel_k[b_ix, k_ix, top_idx].reshape(b, NSA_KV_HEADS, nq, top_n * SEL_BLOCK, NSA_HD)
    gv = sel_v[b_ix, k_ix, top_idx].reshape(b, NSA_KV_HEADS, nq, top_n * SEL_BLOCK, NSA_HD)
    tok_pos = top_idx[..., None] * SEL_BLOCK + jnp.arange(SEL_BLOCK)
    m_sel = ((tok_pos <= q_pos[:, None, None]) & chosen[..., None]).reshape(b, NSA_KV_HEADS, nq, top_n * SEL_BLOCK)
    dist_s = (q_pos[:, None] - tok_pos.reshape(b, NSA_KV_HEADS, nq, top_n * SEL_BLOCK)).astype(jnp.float32)
    s = (jnp.einsum('bqkgd,bkqmd->bkgqm', q, gk).astype(jnp.float32) * NSA_SCALE
         - slopes[None, :, :, None, None] * dist_s[:, :, None])
    p_s = masked_softmax(s, m_sel[:, :, None])
    o_s = jnp.einsum('bkgqm,bkqmd->bqkgd', p_s.astype(gv.dtype), gv)
    start = q_pos[0] - win_pos0
    wk = lax.dynamic_slice_in_dim(win_k, start, WINDOW + nq, axis=1)
    wv = lax.dynamic_slice_in_dim(win_v, start, WINDOW + nq, axis=1)
    k_pos = q_pos[0] - WINDOW + jnp.arange(WINDOW + nq)
    dist = q_pos[:, None] - k_pos[None, :]
    m_win = (dist >= 0) & (dist <= WINDOW) & (k_pos[None, :] >= 0)
    s = (jnp.einsum('bqkgd,bskd->bkgqs', q, wk).astype(jnp.float32) * NSA_SCALE
         - slopes[:, :, None, None] * dist.astype(jnp.float32))
    p_w = masked_softmax(s, m_win)
    o_w = jnp.einsum('bkgqs,bskd->bqkgd', p_w.astype(wv.dtype), wv)
    return jnp.stack([o_c, o_s, o_w], axis=2)


def trunk_layer(x, q_off, past, lw, slopes):
    b, t, _ = x.shape
    pos = q_off + jnp.arange(t, dtype=jnp.int32)
    posf = pos.astype(jnp.float32)
    hn = rmsnorm(x, lw['g_attn'])
    z = hn @ lw['w_in']
    split_points = np.cumsum(IN_WIDTHS)[:-1].tolist()
    cq, ckv, kr, sq, sk, sv, nq, nkv, ng, mg = jnp.split(z, split_points, axis=-1)

    q = (rmsnorm(cq, lw['g_q']) @ lw['w_uq']).reshape(b, t, MLA_HEADS, MLA_NOPE + MLA_ROPE)
    q_lat = jnp.einsum('bthn,rhn->bthr', q[..., :MLA_NOPE], lw['w_uk']) * MLA_SCALE
    q_rope = rope(q[..., MLA_NOPE:], posf) * MLA_SCALE
    mla_rows = jnp.concatenate([rmsnorm(ckv, lw['g_kv']), rope(kr, posf)], axis=-1)
    mla_all = mla_rows if past is None else jnp.concatenate([past['mla'], mla_rows], axis=1)
    o_lat = sweep_queries(functools.partial(mla_attend, mla_all[..., :MLA_KV_LORA], mla_all[..., MLA_KV_LORA:]),
                          pos, q_lat, q_rope)
    o_mla = jnp.einsum('bthr,rhv->bthv', o_lat, lw['w_uv']).reshape(b, t, MLA_HEADS * MLA_V)

    sb_rows = jnp.stack([sk.reshape(b, t, SB_KV_HEADS, SB_HD), sv.reshape(b, t, SB_KV_HEADS, SB_HD)], axis=2)
    sb_all = sb_rows if past is None else jnp.concatenate([past['sb'], sb_rows], axis=1)
    o_sb = sweep_queries(functools.partial(sb_attend, sb_all[:, :, 0], sb_all[:, :, 1]),
                         pos, sq.reshape(b, t, SB_KV_HEADS, SB_GROUP, SB_HD)).reshape(b, t, SB_HEADS * SB_HD)

    nkv = nkv.reshape(b, t, 6, NSA_KV_HEADS, NSA_HD)
    nsa_rows = nkv[:, :, :4]
    win_rows = nkv[:, :, 4:]
    nsa_all = nsa_rows if past is None else jnp.concatenate([past['nsa'], nsa_rows], axis=1)
    kc = pool_blocks(nsa_all[:, :, 0], lw['pool_k'])
    vc = pool_blocks(nsa_all[:, :, 1], lw['pool_v'])
    sel_k = to_sel_blocks(nsa_all[:, :, 2])
    sel_v = to_sel_blocks(nsa_all[:, :, 3])
    if past is None:
        win_all, win_pos0 = win_rows, 0
    else:
        win_all = jnp.concatenate([past['win'], win_rows], axis=1)
        win_pos0 = q_off - past['win'].shape[1]
    win_pad = jnp.pad(win_all, ((0, 0), (WINDOW, 0), (0, 0), (0, 0), (0, 0)))
    o3 = sweep_queries(functools.partial(nsa_attend, kc, vc, sel_k, sel_v, win_pad[:, :, 0], win_pad[:, :, 1],
                                         win_pos0, slopes),
                       pos, nq.reshape(b, t, NSA_KV_HEADS, NSA_GROUP, NSA_HD))
    g3 = jax.nn.sigmoid(ng).reshape(b, t, 3, NSA_KV_HEADS, NSA_GROUP, 1)
    o_nsa = jnp.sum(g3 * o3, axis=2).reshape(b, t, NSA_HEADS * NSA_HD)
    win_state = win_all[:, -min(WINDOW, win_all.shape[1]):]

    gm = jax.nn.sigmoid(mg).reshape(b, t, 3, D_MODEL)
    m = (gm[:, :, 0] * (o_mla @ lw['p_mla']) + gm[:, :, 1] * (o_sb @ lw['p_sb'])
         + gm[:, :, 2] * (o_nsa @ lw['p_nsa']))
    x = x + m @ lw['w_o']

    hf = rmsnorm(x, lw['g_ffn'])
    u = hf @ lw['w_up']
    v = hf @ lw['w_val']
    prev = jnp.zeros((b, CONV_W - 1, D_FF), u.dtype) if past is None else past['conv']
    u_ext = jnp.concatenate([prev, u], axis=1)
    uc = lw['conv_b']
    for k in range(CONV_W):
        uc = uc + lw['conv_w'][k] * u_ext[:, k:k + t]
    x = x + (jax.nn.gelu(uc, approximate=False) * v) @ lw['w_down']
    return x, (mla_rows, sb_rows, nsa_rows, win_state, u_ext[:, -(CONV_W - 1):])


def setup_inputs(seed: int = 0) -> dict:
    key = jax.random.key(seed)
    ks = jax.random.split(key, 32)
    def nrm(i, shape, scale):
        return jax.random.normal(ks[i], shape, jnp.float32) * scale
    def gain(i, shape):
        return 1.0 + nrm(i, shape, 0.02)
    n_pages = PAST_LEN // PAGE_SIZE
    n_used = DEC_BATCH * n_pages
    n_pool = n_used + n_used // 4
    win_buf = min(WINDOW, PAST_LEN)
    page_table = jax.random.permutation(ks[0], n_pool)[:n_used].reshape(DEC_BATCH, n_pages).astype(jnp.int32)
    return {
        'x_prompt': nrm(1, (BATCH, SEQ, D_MODEL), 1.0),
        'x_sample': nrm(2, (DEC_BATCH, DEC_SEQ, D_MODEL), 1.0),
        'cache_mla': nrm(3, (DEPTH, n_pool, PAGE_SIZE, MLA_KV_LORA + MLA_ROPE), 1.0),
        'cache_sb': nrm(4, (DEPTH, n_pool, PAGE_SIZE, 2, SB_KV_HEADS, SB_HD), 1.0),
        'cache_nsa': nrm(5, (DEPTH, n_pool, PAGE_SIZE, 4, NSA_KV_HEADS, NSA_HD), 1.0),
        'state_nsa_win': nrm(6, (DEPTH, DEC_BATCH, win_buf, 2, NSA_KV_HEADS, NSA_HD), 1.0),
        'state_ffn_conv': nrm(7, (DEPTH, DEC_BATCH, CONV_W - 1, D_FF), 1.0),
        'page_table': page_table,
        'g_attn': gain(8, (DEPTH, D_MODEL)),
        'w_in': nrm(9, (DEPTH, D_MODEL, N_IN), D_MODEL ** -0.5),
        'g_q': gain(10, (DEPTH, MLA_Q_LORA)),
        'w_uq': nrm(11, (DEPTH, MLA_Q_LORA, MLA_HEADS * (MLA_NOPE + MLA_ROPE)), MLA_Q_LORA ** -0.5),
        'g_kv': gain(12, (DEPTH, MLA_KV_LORA)),
        'w_uk': nrm(13, (DEPTH, MLA_KV_LORA, MLA_HEADS, MLA_NOPE), MLA_KV_LORA ** -0.5),
        'w_uv': nrm(14, (DEPTH, MLA_KV_LORA, MLA_HEADS, MLA_V), MLA_KV_LORA ** -0.5),
        'nsa_pool_k': (1.0 + nrm(15, (DEPTH, CMP_BLOCK), 0.1)) / CMP_BLOCK,
        'nsa_pool_v': (1.0 + nrm(16, (DEPTH, CMP_BLOCK), 0.1)) / CMP_BLOCK,
        'p_mla': nrm(17, (DEPTH, MLA_HEADS * MLA_V, D_MODEL), (MLA_HEADS * MLA_V) ** -0.5),
        'p_sb': nrm(18, (DEPTH, SB_HEADS * SB_HD, D_MODEL), (SB_HEADS * SB_HD) ** -0.5),
        'p_nsa': nrm(19, (DEPTH, NSA_HEADS * NSA_HD, D_MODEL), (NSA_HEADS * NSA_HD) ** -0.5),
        'w_o': nrm(20, (DEPTH, D_MODEL, D_MODEL), D_MODEL ** -0.5),
        'g_ffn': gain(21, (DEPTH, D_MODEL)),
        'w_up': nrm(22, (DEPTH, D_MODEL, D_FF), D_MODEL ** -0.5),
        'w_val': nrm(23, (DEPTH, D_MODEL, D_FF), D_MODEL ** -0.5),
        'conv_w': nrm(24, (DEPTH, CONV_W, D_FF), CONV_W ** -0.5),
        'conv_b': nrm(25, (DEPTH, D_FF), 0.01),
        'w_down': nrm(26, (DEPTH, D_FF, D_MODEL), D_FF ** -0.5),
        'g_final': gain(27, (D_MODEL,)),
    }


def reference(x_prompt, x_sample, cache_mla, cache_sb, cache_nsa, state_nsa_win, state_ffn_conv, page_table,
              g_attn, w_in, g_q, w_uq, g_kv, w_uk, w_uv, nsa_pool_k, nsa_pool_v, p_mla, p_sb, p_nsa, w_o,
              g_ffn, w_up, w_val, conv_w, conv_b, w_down, g_final):
    slopes = alibi_slopes()
    hp, hs = x_prompt, x_sample
    st_p, st_s = [], []
    for l in range(DEPTH):
        lw = {'g_attn': g_attn[l], 'w_in': w_in[l], 'g_q': g_q[l], 'w_uq': w_uq[l], 'g_kv': g_kv[l],
              'w_uk': w_uk[l], 'w_uv': w_uv[l], 'pool_k': nsa_pool_k[l], 'pool_v': nsa_pool_v[l],
              'p_mla': p_mla[l], 'p_sb': p_sb[l], 'p_nsa': p_nsa[l], 'w_o': w_o[l], 'g_ffn': g_ffn[l],
              'w_up': w_up[l], 'w_val': w_val[l], 'conv_w': conv_w[l], 'conv_b': conv_b[l],
              'w_down': w_down[l]}
        hp, sp = trunk_layer(hp, 0, None, lw, slopes)
        past = {'mla': gather_pages(cache_mla[l], page_table),
                'sb': gather_pages(cache_sb[l], page_table),
                'nsa': gather_pages(cache_nsa[l], page_table),
                'win': state_nsa_win[l], 'conv': state_ffn_conv[l]}
        hs, ss = trunk_layer(hs, PAST_LEN, past, lw, slopes)
        st_p.append(sp)
        st_s.append(ss)
    y_prompt = rmsnorm(hp, g_final)
    y_sample = rmsnorm(hs, g_final)
    new_mla_p = jnp.stack([s[0] for s in st_p])
    new_sb_p = jnp.stack([s[1] for s in st_p])
    new_nsa_p = jnp.stack([s[2] for s in st_p])
    new_win_p = jnp.stack([s[3] for s in st_p])
    new_conv_p = jnp.stack([s[4] for s in st_p])
    new_mla_s = jnp.stack([s[0] for s in st_s])
    new_sb_s = jnp.stack([s[1] for s in st_s])
    new_nsa_s = jnp.stack([s[2] for s in st_s])
    new_win_s = jnp.stack([s[3] for s in st_s])
    new_conv_s = jnp.stack([s[4] for s in st_s])
    return (y_prompt, y_sample, new_mla_p, new_sb_p, new_nsa_p, new_win_p, new_conv_p,
            new_mla_s, new_sb_s, new_nsa_s, new_win_s, new_conv_s)
```

```python
import functools

import numpy as np
import jax
import jax.numpy as jnp
from jax import lax
from jax.experimental import pallas as pl
from jax.experimental.pallas import tpu as pltpu

f32 = jnp.float32
bf16 = jnp.bfloat16
i32 = jnp.int32

D_MODEL = 1024
PAGE = 128
H = 8
Q_LORA, KV_LORA, NOPE, ROPE, VDIM = 256, 128, 64, 32, 64
LAT = KV_LORA + ROPE
ROPE_BASE = 10000.0
MLA_SCALE = (NOPE + ROPE) ** -0.5
SB_KVH, SB_G, HD = 2, 4, 64
SB_SCALE = HD ** -0.5
NSA_SCALE = HD ** -0.5
CMP_BLOCK, SEL_BLOCK, SEL_TOPN, WINDOW = 32, 64, 8, 512
FORCE_BONUS = 1.0e4
D_FF = 2816
NORM_EPS = 1e-6
NEG = -1e30
IN_WIDTHS = (Q_LORA, KV_LORA, ROPE, H * HD, SB_KVH * HD, SB_KVH * HD, H * HD, 6 * HD, 3 * H, 3 * D_MODEL)

ZT_GATE, ZT_NQ, ZT_SQ, ZT_CQ, ZT_NG, ZT_W = 0, 3072, 3584, 4096, 4352, 4480
ZF_SB, ZF_NSA, ZF_WIN, ZF_MLA, ZF_W = 0, 256, 512, 640, 832

VMEM_LIMIT = 56 * 1024 * 1024


def _cparams(*sem):
    return pltpu.CompilerParams(dimension_semantics=sem, vmem_limit_bytes=VMEM_LIMIT)


def _dot(a, b):
    return jnp.dot(a, b, preferred_element_type=f32)


def _dot_nt(a, b):
    return lax.dot_general(a, b, (((1,), (1,)), ((), ())), preferred_element_type=f32)


def _split_dot(a, b, terms, nt=False):
    out = None
    r = a
    for t in range(terms):
        piece = r.astype(bf16)
        d = _dot_nt(piece, b) if nt else _dot(piece, b)
        out = d if out is None else out + d
        if t + 1 < terms:
            r = r - piece.astype(f32)
    return out


def _rms_rows(x, g):
    return x * lax.rsqrt(jnp.mean(x * x, axis=-1, keepdims=True) + NORM_EPS) * g


def _sigmoid(x):
    return 1.0 / (1.0 + jnp.exp(-x))


def _slopes3():
    h = lax.broadcasted_iota(i32, (H, 1, 1), 0)
    s = jnp.full((H, 1, 1), 2.0 ** -H, f32)
    for k in range(H - 1):
        s = jnp.where(h == k, 2.0 ** -(k + 1), s)
    return s


def _stack_heads(q, n):
    return jnp.concatenate([q[:, k * HD:(k + 1) * HD] for k in range(n)], axis=0)


def _masked_softmax(s, mask):
    sm = jnp.where(mask, s, NEG)
    m = jnp.max(sm, axis=-1, keepdims=True)
    p = jnp.where(mask, jnp.exp(sm - m), 0.0)
    l = jnp.sum(p, axis=-1, keepdims=True)
    return p / jnp.where(l > 0.0, l, 1.0)


def _in_proj_kernel(x_ref, g_ref, wt_ref, wf_ref, zt_ref, sb_ref, nsa_ref, win_ref, mla_ref):
    hn = _rms_rows(x_ref[...], g_ref[...]).astype(bf16)
    zt_ref[...] = _dot_nt(hn, wt_ref[...])
    zf = _dot_nt(wf_ref[...], hn)
    sb_ref[...] = zf[ZF_SB:ZF_NSA]
    nsa_ref[...] = zf[ZF_NSA:ZF_WIN]
    win_ref[...] = zf[ZF_WIN:ZF_MLA]
    mla_ref[...] = zf[ZF_MLA:ZF_W]


def in_proj(x, g, wt, wf, nseq, tm):
    n = x.shape[0]
    t = n // nseq
    nt = t // tm
    fm = lambda rows: pl.BlockSpec((None, rows, tm), lambda i: (i // nt, 0, i % nt))
    const = lambda shape: pl.BlockSpec(shape, lambda i: (0,) * len(shape))
    return pl.pallas_call(
        _in_proj_kernel,
        grid=(n // tm,),
        in_specs=[pl.BlockSpec((tm, D_MODEL), lambda i: (i, 0)), const((1, D_MODEL)),
                  const((ZT_W, D_MODEL)), const((ZF_W, D_MODEL))],
        out_specs=[pl.BlockSpec((tm, ZT_W), lambda i: (i, 0)), fm(256), fm(256), fm(128), fm(192)],
        out_shape=[jax.ShapeDtypeStruct((n, ZT_W), f32),
                   jax.ShapeDtypeStruct((nseq, 256, t), f32), jax.ShapeDtypeStruct((nseq, 256, t), f32),
                   jax.ShapeDtypeStruct((nseq, 128, t), f32), jax.ShapeDtypeStruct((nseq, 192, t), f32)],
        compiler_params=_cparams("parallel"),
        name="in_proj",
    )(x, g, wt, wf)


def _mla_prep_kernel(cq_ref, cos_ref, sin_ref, mla_ref, cost_ref, sint_ref, gq_ref, wuq_ref, gkv_ref, wuk_ref,
                     q_ref, rows_ref, kt_ref):
    cqn = _rms_rows(cq_ref[...], gq_ref[...]).astype(bf16)
    qa = _dot(cqn, wuq_ref[...])
    nr = H * NOPE
    qr = (qa[:, nr:nr + H * ROPE] * cos_ref[...] + qa[:, nr + H * ROPE:] * sin_ref[...]) * MLA_SCALE
    for h in range(H):
        ql = _dot(qa[:, h * NOPE:(h + 1) * NOPE].astype(bf16), wuk_ref[h]) * MLA_SCALE
        q_ref[h, :, 0:KV_LORA] = ql.astype(bf16)
        q_ref[h, :, KV_LORA:LAT] = qr[:, h * ROPE:(h + 1) * ROPE].astype(bf16)
    m = mla_ref[...]
    ckv = m[0:KV_LORA]
    ckn = ckv * lax.rsqrt(jnp.mean(ckv * ckv, axis=0, keepdims=True) + NORM_EPS) * gkv_ref[...]
    kr = m[KV_LORA:LAT] * cost_ref[...] + m[LAT:LAT + ROPE] * sint_ref[...]
    rows_ref[0:KV_LORA, :] = ckn
    rows_ref[KV_LORA:LAT, :] = kr
    kt_ref[0:KV_LORA, :] = ckn.astype(bf16)
    kt_ref[KV_LORA:LAT, :] = kr.astype(bf16)


def mla_prep(zt, mla_t, cos8, sin8, cos_t, sin_t, gq, wuq, gkv, wuk, tm):
    n = zt.shape[0]
    nseq, _, t = mla_t.shape
    nt = t // tm
    const = lambda shape: pl.BlockSpec(shape, lambda i: (0,) * len(shape))
    fm = lambda rows: pl.BlockSpec((None, rows, tm), lambda i: (i // nt, 0, i % nt))
    return pl.pallas_call(
        _mla_prep_kernel,
        grid=(n // tm,),
        in_specs=[pl.BlockSpec((tm, Q_LORA), lambda i: (i, ZT_CQ // Q_LORA)),
                  pl.BlockSpec((tm, H * ROPE), lambda i: (i % nt, 0)), pl.BlockSpec((tm, H * ROPE), lambda i: (i % nt, 0)),
                  fm(192),
                  pl.BlockSpec((ROPE, tm), lambda i: (0, i % nt)), pl.BlockSpec((ROPE, tm), lambda i: (0, i % nt)),
                  const((1, Q_LORA)), const((Q_LORA, H * (NOPE + 2 * ROPE))), const((KV_LORA, 1)),
                  const((H, NOPE, KV_LORA))],
        out_specs=[pl.BlockSpec((H, tm, LAT), lambda i: (0, i, 0)), fm(LAT), fm(LAT)],
        out_shape=[jax.ShapeDtypeStruct((H, n, LAT), bf16), jax.ShapeDtypeStruct((nseq, LAT, t), f32),
                   jax.ShapeDtypeStruct((nseq, LAT, t), bf16)],
        compiler_params=_cparams("parallel"),
        name="mla_prep",
    )(zt, cos8, sin8, mla_t, cos_t, sin_t, gq, wuq, gkv, wuk)


def _mla_attn_kernel(q_ref, kt_ref, o_ref, m_ref, l_ref, acc_ref, *, tq, tk):
    i, j = pl.program_id(1), pl.program_id(2)
    r = H * tq

    @pl.when(j == 0)
    def _():
        m_ref[...] = jnp.full_like(m_ref, NEG)
        l_ref[...] = jnp.zeros_like(l_ref)
        acc_ref[...] = jnp.zeros_like(acc_ref)

    @pl.when(j * tk <= i * tq + tq - 1)
    def _():
        q = q_ref[...].reshape(r, LAT)
        kt = kt_ref[...]
        s = _dot(q, kt).reshape(H, tq, tk)
        qpos = i * tq + lax.broadcasted_iota(i32, (1, tq, 1), 1)
        kpos = j * tk + lax.broadcasted_iota(i32, (1, 1, tk), 2)
        s = jnp.where(kpos <= qpos, s, NEG).reshape(r, tk)
        m_prev = m_ref[...]
        m_new = jnp.maximum(m_prev, jnp.max(s, axis=-1, keepdims=True))
        p = jnp.exp(s - m_new)
        alpha = jnp.exp(m_prev - m_new)
        l_ref[...] = alpha * l_ref[...] + jnp.sum(p, axis=-1, keepdims=True)
        acc_ref[...] = alpha * acc_ref[...] + _dot_nt(p.astype(bf16), kt[0:KV_LORA])
        m_ref[...] = m_new

    @pl.when(j == pl.num_programs(2) - 1)
    def _():
        o_ref[...] = (acc_ref[...] / l_ref[...]).reshape(H, tq, KV_LORA).astype(bf16)


def mla_attn_prompt(q, kt, tq, tk):
    nseq, _, t = kt.shape
    nq, nk = t // tq, t // tk
    r = H * tq
    return pl.pallas_call(
        functools.partial(_mla_attn_kernel, tq=tq, tk=tk),
        grid=(nseq, nq, nk),
        in_specs=[pl.BlockSpec((H, tq, LAT), lambda b, i, j: (0, b * nq + i, 0)),
                  pl.BlockSpec((None, LAT, tk), lambda b, i, j: (b, 0, jnp.minimum(j, (i * tq + tq - 1) // tk)))],
        out_specs=pl.BlockSpec((H, tq, KV_LORA), lambda b, i, j: (0, b * nq + i, 0)),
        out_shape=jax.ShapeDtypeStruct((H, nseq * t, KV_LORA), bf16),
        scratch_shapes=[pltpu.VMEM((r, 1), f32), pltpu.VMEM((r, 1), f32), pltpu.VMEM((r, KV_LORA), f32)],
        compiler_params=_cparams("parallel", "parallel", "arbitrary"),
        name="mla_attn",
    )(q, kt)


def _log_sigmoid_pair(z):
    t = jnp.log1p(jnp.exp(-jnp.abs(z)))
    lsp = jnp.minimum(z, 0.0) - t
    return lsp, lsp - z


def _sb_attn_kernel(q_ref, kv_ref, o_ref, qs_ref, carry_ref, acc_ref, *, tq, tk):
    i, kk = pl.program_id(1), pl.program_id(2)
    last = (i * tq + tq - 1) // tk
    jj = last - kk
    r = SB_G * tq

    @pl.when(kk == 0)
    def _():
        q = q_ref[...]
        for g in range(SB_KVH):
            qs_ref[g] = _stack_heads(q[:, g * SB_G * HD:(g + 1) * SB_G * HD], SB_G).astype(bf16)
        carry_ref[...] = jnp.zeros_like(carry_ref)
        acc_ref[...] = jnp.zeros_like(acc_ref)

    @pl.when(jj >= 0)
    def _():
        tri = (lax.broadcasted_iota(i32, (tk, tk), 0) > lax.broadcasted_iota(i32, (tk, tk), 1)).astype(bf16)
        qpos = i * tq + lax.broadcasted_iota(i32, (1, tq, 1), 1)
        kpos = jj * tk + lax.broadcasted_iota(i32, (1, 1, tk), 2)
        mask = kpos < qpos
        for g in range(SB_KVH):
            kt = kv_ref[g * HD:(g + 1) * HD, :].astype(bf16)
            vt = kv_ref[(SB_KVH + g) * HD:(SB_KVH + g + 1) * HD, :].astype(bf16)
            z = (_dot(qs_ref[g], kt) * SB_SCALE).reshape(SB_G, tq, tk)
            lsp, lsn = _log_sigmoid_pair(z)
            lst = jnp.where(mask, lsn, 0.0).reshape(r, tk)
            suffix = _split_dot(lst, tri, 2)
            carry = carry_ref[g]
            later = (suffix + carry).reshape(SB_G, tq, tk)
            a = jnp.where(mask, jnp.exp(lsp + later), 0.0).reshape(r, tk)
            acc_ref[g] = acc_ref[g] + _dot_nt(a.astype(bf16), vt)
            carry_ref[g] = carry + suffix[:, 0:1] + lst[:, 0:1]

    @pl.when(kk == pl.num_programs(2) - 1)
    def _():
        pieces = [acc_ref[g, k * tq:(k + 1) * tq, :] for g in range(SB_KVH) for k in range(SB_G)]
        o_ref[...] = jnp.concatenate(pieces, axis=1).astype(bf16)


def sb_attn_prompt(zt, sb_t, tq, tk):
    nseq, _, t = sb_t.shape
    nq, nk = t // tq, t // tk
    r = SB_G * tq
    kv_map = lambda b, i, kk: (b, 0, jnp.maximum((i * tq + tq - 1) // tk - kk, 0))
    return pl.pallas_call(
        functools.partial(_sb_attn_kernel, tq=tq, tk=tk),
        grid=(nseq, nq, nk),
        in_specs=[pl.BlockSpec((tq, H * HD), lambda b, i, kk: (b * nq + i, ZT_SQ // (H * HD))),
                  pl.BlockSpec((None, 2 * SB_KVH * HD, tk), kv_map)],
        out_specs=pl.BlockSpec((tq, H * HD), lambda b, i, kk: (b * nq + i, 0)),
        out_shape=jax.ShapeDtypeStruct((nseq * t, H * HD), bf16),
        scratch_shapes=[pltpu.VMEM((SB_KVH, r, HD), bf16), pltpu.VMEM((SB_KVH, r, 1), f32),
                        pltpu.VMEM((SB_KVH, r, HD), f32)],
        compiler_params=_cparams("parallel", "parallel", "arbitrary"),
        name="sb_attn",
    )(zt, sb_t)


def _select_blocks(score, blk, n_pick):
    sel = jnp.zeros_like(score)
    sc = score
    for _ in range(n_pick):
        mx = jnp.max(sc, axis=-1, keepdims=True)
        idx = jnp.min(jnp.where(sc == mx, blk, 1e9), axis=-1, keepdims=True)
        pick = blk == idx
        sel = jnp.where(pick & (mx >= 0.0), 1.0, sel)
        sc = jnp.where(pick, -3e38, sc)
    return sel


def _nsa_attn_kernel(q_ref, ng_ref, kv_ref, win_ref, pk_ref, pv_ref, o_ref, kc_ref, vc_ref, m_ref, l_ref, acc_ref,
                     *, t, tq, tk):
    i = pl.program_id(1)
    n_cmp, n_sel = t // CMP_BLOCK, t // SEL_BLOCK
    r = H * tq

    @pl.when(i == 0)
    def _():
        kc_ref[...] = _dot(kv_ref[0:HD, :].astype(bf16), pk_ref[...])
        vc_ref[...] = _dot(kv_ref[HD:2 * HD, :].astype(bf16), pv_ref[...])

    qs = _stack_heads(q_ref[...], H).astype(bf16)
    slope = _slopes3()
    qpos = i * tq + lax.broadcasted_iota(i32, (1, tq, 1), 1)
    qf = qpos.astype(f32)

    cend = (lax.broadcasted_iota(i32, (1, 1, n_cmp), 2) + 1) * CMP_BLOCK - 1
    s = _dot(qs, kc_ref[...].astype(bf16)).reshape(H, tq, n_cmp) * NSA_SCALE - slope * (qf - cend.astype(f32))
    p_c = _masked_softmax(s, cend <= qpos)
    o_c = _dot_nt(p_c.reshape(r, n_cmp).astype(bf16), vc_ref[...].astype(bf16)).reshape(H, tq, HD)

    imp_c = jnp.sum(p_c, axis=0)
    pair = (lax.broadcasted_iota(i32, (n_cmp, n_sel), 0) // (SEL_BLOCK // CMP_BLOCK)
            == lax.broadcasted_iota(i32, (n_cmp, n_sel), 1)).astype(bf16)
    imp = _split_dot(imp_c, pair, 3)
    blk_i = lax.broadcasted_iota(i32, (1, n_sel), 1)
    qp2 = i * tq + lax.broadcasted_iota(i32, (tq, 1), 0)
    cur = qp2 // SEL_BLOCK
    forced = (blk_i == 0) | (blk_i == cur) | (blk_i == cur - 1)
    valid = blk_i * SEL_BLOCK <= qp2
    score = jnp.where(valid, imp + jnp.where(forced, FORCE_BONUS, 0.0), -1.0)
    sel = _select_blocks(score, blk_i.astype(f32), min(SEL_TOPN, n_sel)).astype(bf16)

    m_ref[...] = jnp.full_like(m_ref, NEG)
    l_ref[...] = jnp.zeros_like(l_ref)
    acc_ref[...] = jnp.zeros_like(acc_ref)
    for c in range(t // tk):
        @pl.when(c * tk <= i * tq + tq - 1)
        def _(c=c):
            kt = kv_ref[2 * HD:3 * HD, c * tk:(c + 1) * tk].astype(bf16)
            vt = kv_ref[3 * HD:4 * HD, c * tk:(c + 1) * tk].astype(bf16)
            expand = ((c * tk + lax.broadcasted_iota(i32, (n_sel, tk), 1)) // SEL_BLOCK
                      == lax.broadcasted_iota(i32, (n_sel, tk), 0)).astype(bf16)
            chosen = _dot(sel, expand) > 0.5
            kpos = c * tk + lax.broadcasted_iota(i32, (1, 1, tk), 2)
            mask = chosen[None] & (kpos <= qpos)
            s = _dot(qs, kt).reshape(H, tq, tk) * NSA_SCALE - slope * (qf - kpos.astype(f32))
            sm = jnp.where(mask, s, NEG)
            m_prev = m_ref[...].reshape(H, tq, 1)
            m_new = jnp.maximum(m_prev, jnp.max(sm, axis=-1, keepdims=True))
            p = jnp.where(mask, jnp.exp(sm - m_new), 0.0)
            alpha = jnp.exp(m_prev - m_new)
            l_ref[...] = (alpha * l_ref[...].reshape(H, tq, 1) + jnp.sum(p, axis=-1, keepdims=True)).reshape(r, 1)
            acc_ref[...] = alpha.reshape(r, 1) * acc_ref[...] + _dot_nt(p.reshape(r, tk).astype(bf16), vt)
            m_ref[...] = m_new.reshape(r, 1)
    l = l_ref[...]
    o_s = (acc_ref[...] / jnp.where(l > 0.0, l, 1.0)).reshape(H, tq, HD)

    nw = WINDOW + tq
    start = pl.multiple_of(jnp.maximum(i * tq - WINDOW, 0), PAGE)
    kt = win_ref[0:HD, pl.ds(start, nw)].astype(bf16)
    vt = win_ref[HD:2 * HD, pl.ds(start, nw)].astype(bf16)
    dist = qpos - (start + lax.broadcasted_iota(i32, (1, 1, nw), 2))
    s = _dot(qs, kt).reshape(H, tq, nw) * NSA_SCALE - slope * dist.astype(f32)
    p_w = _masked_softmax(s, (dist >= 0) & (dist <= WINDOW))
    o_w = _dot_nt(p_w.reshape(r, nw).astype(bf16), vt).reshape(H, tq, HD)

    gates = _sigmoid(ng_ref[...])
    col = lambda b: jnp.stack([gates[:, b * H + h:b * H + h + 1] for h in range(H)], axis=0)
    o = col(0) * o_c + col(1) * o_s + col(2) * o_w
    o_ref[...] = jnp.concatenate([o[h] for h in range(H)], axis=1).astype(bf16)


def nsa_attn_prompt(zt, nsa_t, win_t, pool_k, pool_v, tq, tk):
    nseq, _, t = nsa_t.shape
    nq = t // tq
    r = H * tq
    n_cmp = t // CMP_BLOCK
    const = lambda shape: pl.BlockSpec(shape, lambda b, i: (0,) * len(shape))
    return pl.pallas_call(
        functools.partial(_nsa_attn_kernel, t=t, tq=tq, tk=tk),
        grid=(nseq, nq),
        in_specs=[pl.BlockSpec((tq, H * HD), lambda b, i: (b * nq + i, ZT_NQ // (H * HD))),
                  pl.BlockSpec((tq, PAGE), lambda b, i: (b * nq + i, ZT_NG // PAGE)),
                  pl.BlockSpec((None, 4 * HD, t), lambda b, i: (b, 0, 0)),
                  pl.BlockSpec((None, 2 * HD, t), lambda b, i: (b, 0, 0)),
                  const((t, n_cmp)), const((t, n_cmp))],
        out_specs=pl.BlockSpec((tq, H * HD), lambda b, i: (b * nq + i, 0)),
        out_shape=jax.ShapeDtypeStruct((nseq * t, H * HD), bf16),
        scratch_shapes=[pltpu.VMEM((HD, n_cmp), f32), pltpu.VMEM((HD, n_cmp), f32),
                        pltpu.VMEM((r, 1), f32), pltpu.VMEM((r, 1), f32), pltpu.VMEM((r, HD), f32)],
        compiler_params=_cparams("parallel", "arbitrary"),
        name="nsa_attn",
    )(zt, zt, nsa_t, win_t, pool_k, pool_v)


def _merge_kernel(olat_ref, osb_ref, onsa_ref, gate_ref, x_ref, wuv_ref, pm_ref, ps_ref, pn_ref, wo_ref, o_ref):
    o_mla = jnp.concatenate([_dot_nt(olat_ref[h], wuv_ref[h]).astype(bf16) for h in range(H)], axis=1)
    gate = _sigmoid(gate_ref[...])
    m = (gate[:, 0:D_MODEL] * _dot(o_mla, pm_ref[...])
         + gate[:, D_MODEL:2 * D_MODEL] * _dot(osb_ref[...], ps_ref[...])
         + gate[:, 2 * D_MODEL:] * _dot(onsa_ref[...], pn_ref[...]))
    o_ref[...] = x_ref[...] + _dot(m.astype(bf16), wo_ref[...])


def merge(olat, osb, onsa, zt, x, wuv, pm, ps, pn, wo, tm):
    n = x.shape[0]
    const = lambda shape: pl.BlockSpec(shape, lambda i: (0,) * len(shape))
    row = lambda w: pl.BlockSpec((tm, w), lambda i: (i, 0))
    return pl.pallas_call(
        _merge_kernel,
        grid=(n // tm,),
        in_specs=[pl.BlockSpec((H, tm, KV_LORA), lambda i: (0, i, 0)), row(H * HD), row(H * HD), row(3 * D_MODEL),
                  row(D_MODEL), const((H, VDIM, KV_LORA)), const((H * VDIM, D_MODEL)), const((H * HD, D_MODEL)),
                  const((H * HD, D_MODEL)), const((D_MODEL, D_MODEL))],
        out_specs=row(D_MODEL),
        out_shape=jax.ShapeDtypeStruct((n, D_MODEL), f32),
        compiler_params=_cparams("parallel"),
        name="merge",
    )(olat, osb, onsa, zt, x, wuv, pm, ps, pn, wo)


FF_CHUNK = 256


def _gelu(x):
    return 0.5 * x * (1.0 + lax.erf(x * np.float32(np.sqrt(0.5))))


def _ffn_kernel(x_ref, g_ref, wup_ref, wval_ref, cw_ref, cb_ref, wdn_ref, gfin_ref, *rest, seq_mode, final_norm, nt):
    tm = x_ref.shape[0]
    if seq_mode:
        prev_ref, o_ref, st_ref, ubuf_ref, carry_ref = rest
    else:
        prev0_ref, prev1_ref, o_ref, st_ref = rest
    x = x_ref[...]
    hf = _rms_rows(x, g_ref[...]).astype(bf16)
    if seq_mode:
        @pl.when(pl.program_id(0) % nt == 0)
        def _():
            carry_ref[...] = prev_ref[...]
    acc = jnp.zeros((tm, D_MODEL), f32)
    for c in range(D_FF // FF_CHUNK):
        cols = slice(c * FF_CHUNK, (c + 1) * FF_CHUNK)
        u = _dot(hf, wup_ref[:, cols])
        v = _dot(hf, wval_ref[:, cols])
        if seq_mode:
            ubuf_ref[6:8, :] = carry_ref[:, cols]
            ubuf_ref[8:8 + tm, :] = u
            u2 = ubuf_ref[6:6 + tm, :]
            u1 = ubuf_ref[7:7 + tm, :]
            carry_ref[:, cols] = u[tm - 2:tm, :]
        else:
            u2 = prev0_ref[:, cols]
            u1 = prev1_ref[:, cols]
            st_ref[:, cols] = u
        uc = cb_ref[:, cols] + cw_ref[0:1, cols] * u2 + cw_ref[1:2, cols] * u1 + cw_ref[2:3, cols] * u
        acc = acc + _dot((_gelu(uc) * v).astype(bf16), wdn_ref[cols, :])
    if seq_mode:
        st_ref[...] = carry_ref[...]
    y = x + acc
    if final_norm:
        y = _rms_rows(y, gfin_ref[...])
    o_ref[...] = y


def ffn(x, g, wup, wval, cw, cb, wdn, gfin, prev, nseq, tm, final_norm):
    n = x.shape[0]
    seq_mode = not isinstance(prev, tuple)
    nt = (n // nseq) // tm if seq_mode else 1
    const = lambda shape: pl.BlockSpec(shape, lambda i: (0,) * len(shape))
    row = lambda w: pl.BlockSpec((tm, w), lambda i: (i, 0))
    in_specs = [row(D_MODEL), const((1, D_MODEL)), const((D_MODEL, D_FF)), const((D_MODEL, D_FF)), const((3, D_FF)),
                const((1, D_FF)), const((D_FF, D_MODEL)), const((1, D_MODEL))]
    if seq_mode:
        seq = pl.BlockSpec((None, 2, D_FF), lambda i: (i // nt, 0, 0))
        in_specs += [seq]
        out_specs = [row(D_MODEL), seq]
        out_shape = [jax.ShapeDtypeStruct((n, D_MODEL), f32), jax.ShapeDtypeStruct((nseq, 2, D_FF), f32)]
        scratch = [pltpu.VMEM((tm + 8, FF_CHUNK), f32), pltpu.VMEM((2, D_FF), f32)]
        args = (prev,)
    else:
        in_specs += [row(D_FF), row(D_FF)]
        out_specs = [row(D_MODEL), row(D_FF)]
        out_shape = [jax.ShapeDtypeStruct((n, D_MODEL), f32), jax.ShapeDtypeStruct((n, D_FF), f32)]
        scratch = []
        args = prev
    return pl.pallas_call(
        functools.partial(_ffn_kernel, seq_mode=seq_mode, final_norm=final_norm, nt=nt),
        grid=(n // tm,),
        in_specs=in_specs, out_specs=out_specs, out_shape=out_shape, scratch_shapes=scratch,
        compiler_params=_cparams("arbitrary"),
        name="ffn",
    )(x, g, wup, wval, cw, cb, wdn, gfin, *args)


def _page_specs(view_block, layer, npages, pages_per_step, order):
    def spec(p):
        def index(b, c, pt_ref):
            return (layer, pt_ref[b * npages + order(c) * pages_per_step + p]) + (0,) * (len(view_block) - 2)
        return pl.BlockSpec(view_block, index)
    return [spec(p) for p in range(pages_per_step)]


def _mla_dec_kernel(pt_ref, q_ref, new_ref, *refs, npg, nchunk):
    pages, (o_ref, m_ref, l_ref, acc_ref) = refs[:npg], refs[npg:]
    c = pl.program_id(1)

    @pl.when(c == 0)
    def _():
        m_ref[...] = jnp.full_like(m_ref, NEG)
        l_ref[...] = jnp.zeros_like(l_ref)
        acc_ref[...] = jnp.zeros_like(acc_ref)

    q = q_ref[0]
    kts = [pg[...].astype(bf16) for pg in pages]
    s = jnp.concatenate([_dot(q, kt) for kt in kts], axis=1)
    m_prev = m_ref[...]
    m_new = jnp.maximum(m_prev, jnp.max(s, axis=-1, keepdims=True))
    p = jnp.exp(s - m_new)
    alpha = jnp.exp(m_prev - m_new)
    l = alpha * l_ref[...] + jnp.sum(p, axis=-1, keepdims=True)
    p = p.astype(bf16)
    acc = alpha * acc_ref[...]
    for k, kt in enumerate(kts):
        acc = acc + _dot_nt(p[:, k * PAGE:(k + 1) * PAGE], kt[0:KV_LORA])
    m_ref[...] = m_new
    l_ref[...] = l
    acc_ref[...] = acc

    @pl.when(c == nchunk - 1)
    def _():
        knew = new_ref[0].astype(bf16).astype(f32)
        s_new = jnp.sum(q.astype(f32) * knew, axis=-1, keepdims=True)
        m_fin = jnp.maximum(m_new, s_new)
        p_new = jnp.exp(s_new - m_fin)
        a2 = jnp.exp(m_new - m_fin)
        l_fin = a2 * l + p_new
        acc_fin = a2 * acc + p_new.astype(bf16).astype(f32) * knew[:, 0:KV_LORA]
        o_ref[0] = acc_fin / l_fin


def mla_decode(cache_v, layer, pt, q, new_rows, npg):
    bd = q.shape[0]
    npages = pt.shape[0] // bd
    nchunk = npages // npg
    specs = _page_specs((None, None, LAT, PAGE), layer, npages, npg, lambda c: c)
    return pl.pallas_call(
        functools.partial(_mla_dec_kernel, npg=npg, nchunk=nchunk),
        grid_spec=pltpu.PrefetchScalarGridSpec(
            num_scalar_prefetch=1, grid=(bd, nchunk),
            in_specs=[pl.BlockSpec((1, H, LAT), lambda b, c, p: (b, 0, 0)),
                      pl.BlockSpec((1, 1, LAT), lambda b, c, p: (b, 0, 0))] + specs,
            out_specs=pl.BlockSpec((1, H, KV_LORA), lambda b, c, p: (b, 0, 0)),
            scratch_shapes=[pltpu.VMEM((H, 1), f32), pltpu.VMEM((H, 1), f32), pltpu.VMEM((H, KV_LORA), f32)]),
        out_shape=jax.ShapeDtypeStruct((bd, H, KV_LORA), f32),
        compiler_params=_cparams("parallel", "arbitrary"),
        name="mla_dec",
    )(pt, q, new_rows, *([cache_v] * npg))


def _sb_dec_kernel(pt_ref, q_ref, *refs, npg, nchunk):
    pages, (o_ref, carry_ref, acc_ref) = refs[:npg], refs[npg:]
    c = pl.program_id(1)

    @pl.when(c == 0)
    def _():
        carry_ref[...] = jnp.zeros_like(carry_ref)
        acc_ref[...] = jnp.zeros_like(acc_ref)

    q = q_ref[0]
    rowq = lax.broadcasted_iota(i32, q.shape, 0)
    q_g = [jnp.where((rowq // SB_G) == g, q, jnp.zeros_like(q)) for g in range(SB_KVH)]
    lane = lax.broadcasted_iota(i32, (H, PAGE), 1)
    rowa = lax.broadcasted_iota(i32, (H, PAGE), 0)
    carry = carry_ref[...]
    acc = acc_ref[...]
    for p in reversed(range(npg)):
        pg = pages[p]
        z = sum(_dot(q_g[g], pg[0, g].astype(bf16)) for g in range(SB_KVH)) * SB_SCALE
        lsp, lsn = _log_sigmoid_pair(z)
        x = lsn
        k = 1
        while k < PAGE:
            x = x + jnp.where(lane < PAGE - k, pltpu.roll(x, PAGE - k, axis=1), 0.0)
            k *= 2
        a = jnp.exp(lsp + (x - lsn) + carry).astype(bf16)
        for g in range(SB_KVH):
            a_g = jnp.where((rowa // SB_G) == g, a, jnp.zeros_like(a))
            acc = acc + _dot_nt(a_g, pg[1, g].astype(bf16))
        carry = carry + x[:, 0:1]
    carry_ref[...] = carry
    acc_ref[...] = acc

    @pl.when(c == nchunk - 1)
    def _():
        o_ref[0] = acc


def sb_decode(cache_v, layer, pt, q, npg):
    bd = q.shape[0]
    npages = pt.shape[0] // bd
    nchunk = npages // npg
    specs = _page_specs((None, None, 2, SB_KVH, HD, PAGE), layer, npages, npg, lambda c: nchunk - 1 - c)
    return pl.pallas_call(
        functools.partial(_sb_dec_kernel, npg=npg, nchunk=nchunk),
        grid_spec=pltpu.PrefetchScalarGridSpec(
            num_scalar_prefetch=1, grid=(bd, nchunk),
            in_specs=[pl.BlockSpec((1, H, HD), lambda b, c, p: (b, 0, 0))] + specs,
            out_specs=pl.BlockSpec((1, H, HD), lambda b, c, p: (b, 0, 0)),
            scratch_shapes=[pltpu.VMEM((H, 1), f32), pltpu.VMEM((H, HD), f32)]),
        out_shape=jax.ShapeDtypeStruct((bd, H, HD), f32),
        compiler_params=_cparams("parallel", "arbitrary"),
        name="sb_dec",
    )(pt, q, *([cache_v] * npg))


def _nsa_cmp_dec_kernel(pt_ref, q_ref, pool_ref, *refs, npg, nchunk, n_cmp):
    pages, (oc_ref, idx_ref, kc_ref, vc_ref) = refs[:npg], refs[npg:]
    c = pl.program_id(1)
    per_pair = 2 * PAGE // CMP_BLOCK
    for k in range(npg // 2):
        ck = jnp.concatenate([pages[2 * k][0], pages[2 * k + 1][0]], axis=1).astype(bf16)
        cv = jnp.concatenate([pages[2 * k][1], pages[2 * k + 1][1]], axis=1).astype(bf16)
        row0 = pl.multiple_of((c * (npg // 2) + k) * per_pair, per_pair)
        kc_ref[pl.ds(row0, per_pair), :] = _dot_nt(pool_ref[0], ck)
        vc_ref[pl.ds(row0, per_pair), :] = _dot_nt(pool_ref[1], cv)

    @pl.when(c == nchunk - 1)
    def _():
        q = q_ref[0]
        n_sel = n_cmp // 2 + 1
        qposf = jnp.float32(n_cmp * CMP_BLOCK)
        cend = ((lax.broadcasted_iota(i32, (1, n_cmp), 1) + 1) * CMP_BLOCK - 1).astype(f32)
        slope = _slopes3().reshape(H, 1)
        s = _dot_nt(q, kc_ref[...].astype(bf16)) * NSA_SCALE - slope * (qposf - cend)
        m = jnp.max(s, axis=-1, keepdims=True)
        p = jnp.exp(s - m)
        p = p / jnp.sum(p, axis=-1, keepdims=True)
        oc_ref[0] = _dot(p.astype(bf16), vc_ref[...].astype(bf16))
        imp_c = jnp.sum(p, axis=0, keepdims=True)
        nl = idx_ref.shape[-1]
        pair = (lax.broadcasted_iota(i32, (n_cmp, nl), 0) // 2 == lax.broadcasted_iota(i32, (n_cmp, nl), 1)).astype(bf16)
        imp = _split_dot(jnp.broadcast_to(imp_c, (H, n_cmp)), pair, 3)[0:1]
        blk = lax.broadcasted_iota(i32, (1, nl), 1)
        cur = n_sel - 1
        forced = (blk == 0) | (blk == cur) | (blk == cur - 1)
        score = jnp.where(blk < n_sel, imp + jnp.where(forced, FORCE_BONUS, 0.0), -1.0)
        blkf = blk.astype(f32)
        out = jnp.full((1, nl), -1.0, f32)
        sc = score
        for k in range(min(SEL_TOPN, n_sel)):
            mx = jnp.max(sc, axis=-1, keepdims=True)
            idx = jnp.min(jnp.where(sc == mx, blkf, 1e9), axis=-1, keepdims=True)
            out = jnp.where((blk == k) & (mx >= 0.0), idx, out)
            sc = jnp.where(blkf == idx, -3e38, sc)
        idx_ref[0] = out.astype(i32)


def nsa_cmp_decode(cache_v, layer, pt, q, pool2, npg):
    bd = q.shape[0]
    npages = pt.shape[0] // bd
    nchunk = npages // npg
    n_cmp = npages * PAGE // CMP_BLOCK
    nl = max(PAGE, -(-(n_cmp // 2 + 1) // PAGE) * PAGE)
    specs = _page_specs((None, None, 2, HD, PAGE), layer, npages, npg, lambda c: c)
    return pl.pallas_call(
        functools.partial(_nsa_cmp_dec_kernel, npg=npg, nchunk=nchunk, n_cmp=n_cmp),
        grid_spec=pltpu.PrefetchScalarGridSpec(
            num_scalar_prefetch=1, grid=(bd, nchunk),
            in_specs=[pl.BlockSpec((1, H, HD), lambda b, c, p: (b, 0, 0)),
                      pl.BlockSpec((2, 2 * PAGE // CMP_BLOCK, 2 * PAGE), lambda b, c, p: (0, 0, 0))] + specs,
            out_specs=[pl.BlockSpec((1, H, HD), lambda b, c, p: (b, 0, 0)),
                       pl.BlockSpec((1, 1, nl), lambda b, c, p: (b, 0, 0))],
            scratch_shapes=[pltpu.VMEM((n_cmp, HD), f32), pltpu.VMEM((n_cmp, HD), f32)]),
        out_shape=[jax.ShapeDtypeStruct((bd, H, HD), f32), jax.ShapeDtypeStruct((bd, 1, nl), i32)],
        compiler_params=_cparams("parallel", "arbitrary"),
        name="nsa_cmp_dec",
    )(pt, q, pool2, *([cache_v] * npg))


def _nsa_sel_dec_kernel(phys_ref, blk_ref, q_ref, new_ref, oc_ref, ng_ref, win_ref, *refs, nsl, past):
    pages, (o_ref,) = refs[:nsl], refs[nsl:]
    b = pl.program_id(0)
    q = q_ref[0]
    qf32 = q.astype(f32)
    slope = _slopes3().reshape(H, 1)
    new = new_ref[0].astype(bf16).astype(f32)
    lane = lax.broadcasted_iota(i32, (1, PAGE), 1)

    ss, masks = [], []
    for j in range(nsl):
        blk = blk_ref[b * nsl + j]
        tok = (blk // 2) * PAGE + lane
        valid = (blk >= 0) & (blk * SEL_BLOCK < past) & (lane // SEL_BLOCK == blk % 2)
        s = _dot(q, pages[j][0].astype(bf16)) * NSA_SCALE - slope * (past - tok).astype(f32)
        ss.append(jnp.where(valid, s, NEG))
        masks.append(valid)
    s_new = jnp.sum(qf32 * new[2:3], axis=-1, keepdims=True) * NSA_SCALE
    m = jnp.maximum(functools.reduce(jnp.maximum, [jnp.max(s, axis=-1, keepdims=True) for s in ss]), s_new)
    p_new = jnp.exp(s_new - m)
    l = p_new
    acc = p_new.astype(bf16).astype(f32) * new[3:4]
    for j in range(nsl):
        p = jnp.where(masks[j], jnp.exp(ss[j] - m), 0.0)
        l = l + jnp.sum(p, axis=-1, keepdims=True)
        acc = acc + _dot_nt(p.astype(bf16), pages[j][1].astype(bf16))
    o_s = acc / l

    nwin = win_ref.shape[-1]
    dist = (nwin - lax.broadcasted_iota(i32, (1, nwin), 1)).astype(f32)
    s = _dot(q, win_ref[0].astype(bf16)) * NSA_SCALE - slope * dist
    s_new = jnp.sum(qf32 * new[4:5], axis=-1, keepdims=True) * NSA_SCALE
    m = jnp.maximum(jnp.max(s, axis=-1, keepdims=True), s_new)
    p = jnp.exp(s - m)
    p_new = jnp.exp(s_new - m)
    l = jnp.sum(p, axis=-1, keepdims=True) + p_new
    o_w = (_dot_nt(p.astype(bf16), win_ref[1].astype(bf16)) + p_new.astype(bf16).astype(f32) * new[5:6]) / l

    gates = _sigmoid(ng_ref[0])
    o_ref[0] = gates[0] * oc_ref[0] + gates[1] * o_s + gates[2] * o_w


def nsa_sel_decode(cache_v, win_v, layer, phys, blks, q, new6, o_c, ng3, past):
    bd = q.shape[0]
    nsl = phys.shape[0] // bd
    nwin = win_v.shape[-1]

    def page_spec(j):
        return pl.BlockSpec((None, None, 2, HD, PAGE), lambda b, ph, bl: (layer, ph[b * nsl + j], 1, 0, 0))

    per_seq = lambda shape: pl.BlockSpec((1,) + shape, lambda b, ph, bl: (b,) + (0,) * len(shape))
    return pl.pallas_call(
        functools.partial(_nsa_sel_dec_kernel, nsl=nsl, past=past),
        grid_spec=pltpu.PrefetchScalarGridSpec(
            num_scalar_prefetch=2, grid=(bd,),
            in_specs=[per_seq((H, HD)), per_seq((6, HD)), per_seq((H, HD)), per_seq((3, H, 1)),
                      pl.BlockSpec((None, None, 2, HD, nwin), lambda b, ph, bl: (layer, b, 0, 0, 0))]
                     + [page_spec(j) for j in range(nsl)],
            out_specs=per_seq((H, HD))),
        out_shape=jax.ShapeDtypeStruct((bd, H, HD), f32),
        compiler_params=_cparams("parallel"),
        name="nsa_sel_dec",
    )(phys, blks, q, new6, o_c, ng3, win_v, *([cache_v] * nsl))


def _prep_params(w_in, w_uq, w_uk, w_uv, pool_k, pool_v, p_mla, p_sb, p_nsa, w_o, w_up, w_val, w_down, t_prompt):
    depth = w_in.shape[0]
    off = np.concatenate([[0], np.cumsum(IN_WIDTHS)])
    wt = jnp.swapaxes(w_in, 1, 2)
    seg = lambda k: wt[:, off[k]:off[k + 1]]
    cq, ckv, kr, sq, sk, sv, nq, nkv, ng, mg = [seg(k) for k in range(10)]
    half = ROPE // 2
    kr_sw = jnp.concatenate([-kr[:, half:], kr[:, :half]], axis=1)
    zpad = jnp.zeros((depth, ZT_W - ZT_NG - 3 * H, D_MODEL), f32)
    w_tok = jnp.concatenate([mg, nq, sq, cq, ng, zpad], axis=1).astype(bf16)
    w_feat = jnp.concatenate([sk, sv, nkv, ckv, kr, kr_sw], axis=1).astype(bf16)
    uq = w_uq.reshape(depth, Q_LORA, H, NOPE + ROPE)
    uq_r = uq[..., NOPE:]
    uq_sw = jnp.concatenate([-uq_r[..., half:], uq_r[..., :half]], axis=-1)
    wuq = jnp.concatenate([uq[..., :NOPE].reshape(depth, Q_LORA, -1), uq_r.reshape(depth, Q_LORA, -1),
                           uq_sw.reshape(depth, Q_LORA, -1)], axis=-1).astype(bf16)
    wuk = jnp.transpose(w_uk, (0, 2, 3, 1)).astype(bf16)
    wuv = jnp.transpose(w_uv, (0, 2, 3, 1)).astype(bf16)
    tok = np.arange(t_prompt)
    onehot = jnp.asarray((tok[:, None] // CMP_BLOCK == np.arange(t_prompt // CMP_BLOCK)[None, :]), f32)
    pool_p = lambda w: (onehot[None] * jnp.tile(w, (1, t_prompt // CMP_BLOCK))[:, :, None]).astype(bf16)
    tok2 = np.arange(2 * PAGE)
    onehot2 = jnp.asarray((np.arange(2 * PAGE // CMP_BLOCK)[:, None] == tok2[None, :] // CMP_BLOCK), f32)
    pool_d = lambda w: onehot2[None] * jnp.tile(w, (1, 2 * PAGE // CMP_BLOCK))[:, None, :]
    pool2 = jnp.stack([pool_d(pool_k), pool_d(pool_v)], axis=1).astype(bf16)
    cast = lambda a: a.astype(bf16)
    return dict(w_tok=w_tok, w_feat=w_feat, wuq=wuq, wuk=wuk, wuv=wuv, pool_pk=pool_p(pool_k), pool_pv=pool_p(pool_v),
                pool2=pool2, p_mla=cast(p_mla), p_sb=cast(p_sb), p_nsa=cast(p_nsa), w_o=cast(w_o), w_up=cast(w_up),
                w_val=cast(w_val), w_down=cast(w_down))


def _rope_tables(pos):
    half = ROPE // 2
    inv = ROPE_BASE ** (-jnp.arange(half, dtype=f32) / half)
    ang = pos.astype(f32)[:, None] * inv[None, :]
    cos = jnp.concatenate([jnp.cos(ang)] * 2, axis=1)
    sin = jnp.concatenate([jnp.sin(ang)] * 2, axis=1)
    return jnp.tile(cos, (1, H)), jnp.tile(sin, (1, H)), cos.T, sin.T


def kernel(x_prompt, x_sample, cache_mla, cache_sb, cache_nsa, state_nsa_win, state_ffn_conv, page_table, g_attn, w_in, g_q, w_uq, g_kv, w_uk, w_uv, nsa_pool_k, nsa_pool_v, p_mla, p_sb, p_nsa, w_o, g_ffn, w_up, w_val, conv_w, conv_b, w_down, g_final):
    depth = w_in.shape[0]
    bp, t, _ = x_prompt.shape
    bd = x_sample.shape[0]
    npages = page_table.shape[1]
    past = npages * PAGE
    nwin = state_nsa_win.shape[2]
    assert x_sample.shape[1] == 1 and t % 256 == 0 and t >= WINDOW + 128 and bd % 8 == 0 and npages % 8 == 0
    assert nwin == WINDOW and past >= WINDOW

    w = _prep_params(w_in, w_uq, w_uk, w_uv, nsa_pool_k, nsa_pool_v, p_mla, p_sb, p_nsa, w_o, w_up, w_val, w_down, t)
    cos8_p, sin8_p, cost_p, sint_p = _rope_tables(jnp.arange(t))
    cos8_d, sin8_d, cost_d, sint_d = _rope_tables(jnp.full((bd,), past))
    mla_v = jnp.swapaxes(cache_mla, 2, 3)
    sb_v = jnp.transpose(cache_sb, (0, 1, 3, 4, 5, 2))
    nsa_v = jnp.transpose(cache_nsa, (0, 1, 3, 4, 5, 2)).reshape(depth, -1, 4, HD, PAGE)
    win_v = jnp.transpose(state_nsa_win, (0, 1, 3, 4, 5, 2)).reshape(depth, bd, 2, HD, nwin)
    pt = page_table.reshape(-1)
    row2 = lambda a: a.reshape(depth, 1, -1)
    gfin = g_final.reshape(1, -1)

    hp = x_prompt.reshape(bp * t, D_MODEL)
    hs = x_sample.reshape(bd, D_MODEL)
    st_p = [[] for _ in range(5)]
    st_s = [[] for _ in range(5)]
    zero_prev = jnp.zeros((bp, 2, D_FF), f32)
    for l in range(depth):
        last = l == depth - 1
        ga, gq, gf, cb = row2(g_attn)[l], row2(g_q)[l], row2(g_ffn)[l], row2(conv_b)[l]
        gkv = g_kv[l].reshape(-1, 1)
        cw = conv_w[l]

        zt, sb_t, nsa_t, win_t, mla_t = in_proj(hp, ga, w["w_tok"][l], w["w_feat"][l], bp, 256)
        q, rows_t, kt = mla_prep(zt, mla_t, cos8_p, sin8_p, cost_p, sint_p, gq, w["wuq"][l], gkv, w["wuk"][l], 256)
        o_lat = mla_attn_prompt(q, kt, 256, 512)
        o_sb = sb_attn_prompt(zt, sb_t, 128, 256)
        o_nsa = nsa_attn_prompt(zt, nsa_t, win_t, w["pool_pk"][l], w["pool_pv"][l], 128, 512)
        hp = merge(o_lat, o_sb, o_nsa, zt, hp, w["wuv"][l], w["p_mla"][l], w["p_sb"][l], w["p_nsa"][l], w["w_o"][l], 256)
        hp, conv_p = ffn(hp, gf, w["w_up"][l], w["w_val"][l], cw, cb, w["w_down"][l], gfin, zero_prev, bp, 256, last)
        for k, a in enumerate((rows_t, sb_t, nsa_t, win_t[:, :, t - WINDOW:], conv_p)):
            st_p[k].append(a)

        zt, sb_n, nsa_n, win_n, mla_t = in_proj(hs, ga, w["w_tok"][l], w["w_feat"][l], 1, bd)
        q, rows_n, _ = mla_prep(zt, mla_t, cos8_d, sin8_d, cost_d, sint_d, gq, w["wuq"][l], gkv, w["wuk"][l], bd)
        rows_n, sb_n, nsa_n, win_n = (jnp.swapaxes(a[0], 0, 1) for a in (rows_n, sb_n, nsa_n, win_n))
        o_lat = mla_decode(mla_v, l, pt, jnp.swapaxes(q, 0, 1), rows_n.reshape(bd, 1, LAT), 8)
        o_lat = jnp.swapaxes(o_lat, 0, 1).astype(bf16)
        q_sb = zt[:, ZT_SQ:ZT_SQ + H * HD].reshape(bd, H, HD).astype(bf16)
        o_sb = sb_decode(sb_v, l, pt, q_sb, 8).reshape(bd, H * HD).astype(bf16)
        q_nsa = zt[:, ZT_NQ:ZT_NQ + H * HD].reshape(bd, H, HD).astype(bf16)
        o_c, idx = nsa_cmp_decode(nsa_v, l, pt, q_nsa, w["pool2"][l], 8)
        blks = idx[:, 0, :SEL_TOPN]
        phys = jnp.take_along_axis(page_table, jnp.clip(blks // 2, 0, npages - 1), axis=1)
        new6 = jnp.concatenate([nsa_n, win_n], axis=1).reshape(bd, 6, HD)
        ng3 = zt[:, ZT_NG:ZT_NG + 3 * H].reshape(bd, 3, H, 1)
        o_nsa = nsa_sel_decode(nsa_v, win_v, l, phys.reshape(-1), blks.reshape(-1), q_nsa, new6, o_c, ng3, past)
        o_nsa = o_nsa.reshape(bd, H * HD).astype(bf16)
        hs = merge(o_lat, o_sb, o_nsa, zt, hs, w["wuv"][l], w["p_mla"][l], w["p_sb"][l], w["p_nsa"][l], w["w_o"][l], bd)
        prev = (state_ffn_conv[l, :, 0], state_ffn_conv[l, :, 1])
        hs, u_new = ffn(hs, gf, w["w_up"][l], w["w_val"][l], cw, cb, w["w_down"][l], gfin, prev, bd, bd, last)
        win_s = jnp.concatenate([win_v[l][..., 1:], win_n.reshape(bd, 2, HD, 1)], axis=-1)
        for k, a in enumerate((rows_n, sb_n, nsa_n, win_s, jnp.stack([prev[1], u_new], axis=1))):
            st_s[k].append(a)

    stack = lambda xs: jnp.stack(xs)
    fm_to_tm = lambda a, shape: jnp.moveaxis(a, -1, 2).reshape(shape)
    y_prompt = hp.reshape(bp, t, D_MODEL)
    y_sample = hs.reshape(bd, 1, D_MODEL)
    new_mla_p = fm_to_tm(stack(st_p[0]), (depth, bp, t, LAT))
    new_sb_p = fm_to_tm(stack(st_p[1]), (depth, bp, t, 2, SB_KVH, HD))
    new_nsa_p = fm_to_tm(stack(st_p[2]), (depth, bp, t, 4, 1, HD))
    new_win_p = fm_to_tm(stack(st_p[3]), (depth, bp, WINDOW, 2, 1, HD))
    new_conv_p = stack(st_p[4])
    new_mla_s = stack(st_s[0]).reshape(depth, bd, 1, LAT)
    new_sb_s = stack(st_s[1]).reshape(depth, bd, 1, 2, SB_KVH, HD)
    new_nsa_s = stack(st_s[2]).reshape(depth, bd, 1, 4, 1, HD)
    new_win_s = jnp.moveaxis(stack(st_s[3]), -1, 2).reshape(depth, bd, WINDOW, 2, 1, HD)
    new_conv_s = stack(st_s[4])
    return (y_prompt, y_sample, new_mla_p, new_sb_p, new_nsa_p, new_win_p, new_conv_p,
            new_mla_s, new_sb_s, new_nsa_s, new_win_s, new_conv_s)
```

```python
import functools

import numpy as np
import jax
import jax.numpy as jnp
from jax import lax
from jax.experimental import pallas as pl
from jax.experimental.pallas import tpu as pltpu

f32 = jnp.float32
bf16 = jnp.bfloat16
i32 = jnp.int32

D_MODEL = 1024
PAGE = 128
H = 8
Q_LORA, KV_LORA, NOPE, ROPE, VDIM = 256, 128, 64, 32, 64
LAT = KV_LORA + ROPE
KT_ROWS = LAT + 8
ROPE_BASE = 10000.0
MLA_SCALE = (NOPE + ROPE) ** -0.5
SB_KVH, SB_G, HD = 2, 4, 64
SB_SCALE = HD ** -0.5
NSA_SCALE = HD ** -0.5
CMP_BLOCK, SEL_BLOCK, SEL_TOPN, WINDOW = 32, 64, 8, 512
FORCE_BONUS = 1.0e4
D_FF = 2816
NORM_EPS = 1e-6
NEG = -1e30
IN_WIDTHS = (Q_LORA, KV_LORA, ROPE, H * HD, SB_KVH * HD, SB_KVH * HD, H * HD, 6 * HD, 3 * H, 3 * D_MODEL)

ZT_GATE, ZT_NQ, ZT_SQ, ZT_CQ, ZT_NG, ZT_W = 0, 3072, 3584, 4096, 4352, 4480
ZF_SB, ZF_NSA, ZF_WIN, ZF_MLA, ZF_W = 0, 256, 512, 640, 832

VMEM_LIMIT = 56 * 1024 * 1024


def _cparams(*sem):
    return pltpu.CompilerParams(dimension_semantics=sem, vmem_limit_bytes=VMEM_LIMIT)


def _dot(a, b):
    return jnp.dot(a, b, preferred_element_type=f32)


def _dot_nt(a, b):
    return lax.dot_general(a, b, (((1,), (1,)), ((), ())), preferred_element_type=f32)


def _split_dot(a, b, terms):
    out = None
    r = a
    for t in range(terms):
        piece = r.astype(bf16)
        d = _dot(piece, b)
        out = d if out is None else out + d
        if t + 1 < terms:
            r = r - piece.astype(f32)
    return out


def _rms_rows(x, g):
    return x * lax.rsqrt(jnp.mean(x * x, axis=-1, keepdims=True) + NORM_EPS) * g


def _sigmoid(x):
    return 1.0 / (1.0 + jnp.exp(-x))


def _slopes3():
    h = lax.broadcasted_iota(i32, (H, 1, 1), 0)
    s = jnp.full((H, 1, 1), 2.0 ** -H, f32)
    for k in range(H - 1):
        s = jnp.where(h == k, 2.0 ** -(k + 1), s)
    return s


def _stack_heads(q, n):
    return jnp.concatenate([q[:, k * HD:(k + 1) * HD] for k in range(n)], axis=0)


def _masked_softmax(s, mask):
    sm = jnp.where(mask, s, NEG)
    m = jnp.max(sm, axis=-1, keepdims=True)
    p = jnp.where(mask, jnp.exp(sm - m), 0.0)
    l = jnp.sum(p, axis=-1, keepdims=True)
    return p / jnp.where(l > 0.0, l, 1.0)


def _in_proj_kernel(x_ref, g_ref, wt_ref, wf_ref, zt_ref, sb_ref, nsa_ref, win_ref, mla_ref):
    hn = _rms_rows(x_ref[...], g_ref[...]).astype(bf16)
    zt_ref[...] = _dot_nt(hn, wt_ref[...])
    zf = _dot_nt(wf_ref[...], hn)
    sb_ref[...] = zf[ZF_SB:ZF_NSA]
    nsa_ref[...] = zf[ZF_NSA:ZF_WIN]
    win_ref[...] = zf[ZF_WIN:ZF_MLA]
    mla_ref[...] = zf[ZF_MLA:ZF_W]


def in_proj(x, g, wt, wf, nseq, tm):
    n = x.shape[0]
    t = n // nseq
    nt = t // tm
    fm = lambda rows: pl.BlockSpec((None, rows, tm), lambda i: (i // nt, 0, i % nt))
    const = lambda shape: pl.BlockSpec(shape, lambda i: (0,) * len(shape))
    return pl.pallas_call(
        _in_proj_kernel,
        grid=(n // tm,),
        in_specs=[pl.BlockSpec((tm, D_MODEL), lambda i: (i, 0)), const((1, D_MODEL)),
                  const((ZT_W, D_MODEL)), const((ZF_W, D_MODEL))],
        out_specs=[pl.BlockSpec((tm, ZT_W), lambda i: (i, 0)), fm(256), fm(256), fm(128), fm(192)],
        out_shape=[jax.ShapeDtypeStruct((n, ZT_W), f32),
                   jax.ShapeDtypeStruct((nseq, 256, t), f32), jax.ShapeDtypeStruct((nseq, 256, t), f32),
                   jax.ShapeDtypeStruct((nseq, 128, t), f32), jax.ShapeDtypeStruct((nseq, 192, t), f32)],
        compiler_params=_cparams("parallel"),
        name="in_proj",
    )(x, g, wt, wf)


def _mla_prep_kernel(cq_ref, cos_ref, sin_ref, mla_ref, cost_ref, sint_ref, gq_ref, wuq_ref, gkv_ref, wuk_ref,
                     q_ref, rows_ref, kt_ref):
    cqn = _rms_rows(cq_ref[...], gq_ref[...]).astype(bf16)
    qa = _dot(cqn, wuq_ref[...])
    nr = H * NOPE
    qr = (qa[:, nr:nr + H * ROPE] * cos_ref[...] + qa[:, nr + H * ROPE:] * sin_ref[...]) * MLA_SCALE
    for h in range(H):
        ql = _dot(qa[:, h * NOPE:(h + 1) * NOPE].astype(bf16), wuk_ref[h]) * MLA_SCALE
        q_ref[h, :, 0:KV_LORA] = ql.astype(bf16)
        q_ref[h, :, KV_LORA:LAT] = qr[:, h * ROPE:(h + 1) * ROPE].astype(bf16)
    m = mla_ref[...]
    ckv = m[0:KV_LORA]
    ckn = ckv * lax.rsqrt(jnp.mean(ckv * ckv, axis=0, keepdims=True) + NORM_EPS) * gkv_ref[...]
    kr = m[KV_LORA:LAT] * cost_ref[...] + m[LAT:LAT + ROPE] * sint_ref[...]
    rows_ref[0:KV_LORA, :] = ckn
    rows_ref[KV_LORA:LAT, :] = kr
    kt_ref[0:KV_LORA, :] = ckn.astype(bf16)
    kt_ref[KV_LORA:LAT, :] = kr.astype(bf16)
    tail = kt_ref.shape[0] - LAT
    rowi = lax.broadcasted_iota(i32, (tail, kt_ref.shape[1]), 0)
    kt_ref[LAT:, :] = jnp.where(rowi == 0, 1.0, 0.0).astype(bf16)


def mla_prep(zt, mla_t, cos8, sin8, cos_t, sin_t, gq, wuq, gkv, wuk, tm):
    n = zt.shape[0]
    nseq, _, t = mla_t.shape
    nt = t // tm
    const = lambda shape: pl.BlockSpec(shape, lambda i: (0,) * len(shape))
    fm = lambda rows: pl.BlockSpec((None, rows, tm), lambda i: (i // nt, 0, i % nt))
    return pl.pallas_call(
        _mla_prep_kernel,
        grid=(n // tm,),
        in_specs=[pl.BlockSpec((tm, Q_LORA), lambda i: (i, ZT_CQ // Q_LORA)),
                  pl.BlockSpec((tm, H * ROPE), lambda i: (i % nt, 0)), pl.BlockSpec((tm, H * ROPE), lambda i: (i % nt, 0)),
                  fm(192),
                  pl.BlockSpec((ROPE, tm), lambda i: (0, i % nt)), pl.BlockSpec((ROPE, tm), lambda i: (0, i % nt)),
                  const((1, Q_LORA)), const((Q_LORA, H * (NOPE + 2 * ROPE))), const((KV_LORA, 1)),
                  const((H, NOPE, KV_LORA))],
        out_specs=[pl.BlockSpec((H, tm, LAT), lambda i: (0, i, 0)), fm(LAT), fm(KT_ROWS)],
        out_shape=[jax.ShapeDtypeStruct((H, n, LAT), bf16), jax.ShapeDtypeStruct((nseq, LAT, t), f32),
                   jax.ShapeDtypeStruct((nseq, KT_ROWS, t), bf16)],
        compiler_params=_cparams("parallel"),
        name="mla_prep",
    )(zt, cos8, sin8, mla_t, cos_t, sin_t, gq, wuq, gkv, wuk)


def _mla_attn_kernel(q_ref, kt_ref, o_ref, m_ref, acc_ref, *, tq, tk):
    i = pl.program_id(1)
    r = H * tq
    m_ref[...] = jnp.full_like(m_ref, NEG)
    acc_ref[...] = jnp.zeros_like(acc_ref)
    q = q_ref[...].reshape(r, LAT)

    def update(j, masked):
        kt = kt_ref[:, pl.ds(pl.multiple_of(j * tk, tk), tk)]
        s = _dot(q, kt[0:LAT])
        if masked:
            qpos = i * tq + lax.broadcasted_iota(i32, (1, tq, 1), 1)
            kpos = j * tk + lax.broadcasted_iota(i32, (1, 1, tk), 2)
            s = jnp.where(kpos <= qpos, s.reshape(H, tq, tk), NEG).reshape(r, tk)
        m_prev = m_ref[...]
        m_new = jnp.maximum(m_prev, jnp.max(s, axis=-1, keepdims=True))
        p = jnp.exp(s - m_new)
        acc_ref[...] = jnp.exp(m_prev - m_new) * acc_ref[...] + _dot_nt(p.astype(bf16), kt)
        m_ref[...] = m_new

    nfull = (i * tq) // tk

    def body(j, carry):
        update(j, False)
        return carry

    lax.fori_loop(0, nfull, body, 0)
    update(nfull, True)
    acc = acc_ref[...]
    o_ref[...] = (acc[:, 0:KV_LORA] / acc[:, LAT:LAT + 1]).reshape(H, tq, KV_LORA).astype(bf16)


def mla_attn_prompt(q, kt, tq, tk):
    nseq, _, t = kt.shape
    nq = t // tq
    r = H * tq
    assert tk % tq == 0 and t % tk == 0
    return pl.pallas_call(
        functools.partial(_mla_attn_kernel, tq=tq, tk=tk),
        grid=(nseq, nq),
        in_specs=[pl.BlockSpec((H, tq, LAT), lambda b, i: (0, b * nq + i, 0)),
                  pl.BlockSpec((None, KT_ROWS, t), lambda b, i: (b, 0, 0))],
        out_specs=pl.BlockSpec((H, tq, KV_LORA), lambda b, i: (0, b * nq + i, 0)),
        out_shape=jax.ShapeDtypeStruct((H, nseq * t, KV_LORA), bf16),
        scratch_shapes=[pltpu.VMEM((r, 1), f32), pltpu.VMEM((r, KT_ROWS), f32)],
        compiler_params=_cparams("parallel", "parallel"),
        name="mla_attn",
    )(q, kt)


def _log_sigmoid_pair(z):
    t = jnp.log1p(jnp.exp(-jnp.abs(z)))
    lsp = jnp.minimum(z, 0.0) - t
    return lsp, lsp - z


def _sb_attn_kernel(q_ref, kv_ref, o_ref, carry_ref, acc_ref, *, tq, tk):
    i = pl.program_id(1)
    r = SB_G * tq
    q = q_ref[...] * SB_SCALE
    qs = [_stack_heads(q[:, g * SB_G * HD:(g + 1) * SB_G * HD], SB_G).astype(bf16) for g in range(SB_KVH)]
    carry_ref[...] = jnp.zeros_like(carry_ref)
    acc_ref[...] = jnp.zeros_like(acc_ref)
    tri = (lax.broadcasted_iota(i32, (tk, tk), 0) > lax.broadcasted_iota(i32, (tk, tk), 1)).astype(bf16)

    def block(jj, masked):
        cols = pl.ds(pl.multiple_of(jj * tk, tk), tk)
        if masked:
            qpos = i * tq + lax.broadcasted_iota(i32, (1, tq, 1), 1)
            kpos = jj * tk + lax.broadcasted_iota(i32, (1, 1, tk), 2)
            mask = kpos < qpos
        for g in range(SB_KVH):
            kt = kv_ref[g * HD:(g + 1) * HD, cols].astype(bf16)
            vt = kv_ref[(SB_KVH + g) * HD:(SB_KVH + g + 1) * HD, cols].astype(bf16)
            lsp, lsn = _log_sigmoid_pair(_dot(qs[g], kt))
            if masked:
                lsn = jnp.where(mask, lsn.reshape(SB_G, tq, tk), 0.0).reshape(r, tk)
            suffix = _split_dot(lsn, tri, 2)
            carry = carry_ref[g]
            a = jnp.exp(lsp + suffix + carry)
            if masked:
                a = jnp.where(mask, a.reshape(SB_G, tq, tk), 0.0).reshape(r, tk)
            acc_ref[g] = acc_ref[g] + _dot_nt(a.astype(bf16), vt)
            carry_ref[g] = carry + suffix[:, 0:1] + lsn[:, 0:1]

    last = (i * tq + tq - 1) // tk
    block(last, True)

    def body(n, c):
        block(last - 1 - n, False)
        return c

    lax.fori_loop(0, last, body, 0)
    pieces = [acc_ref[g, k * tq:(k + 1) * tq, :] for g in range(SB_KVH) for k in range(SB_G)]
    o_ref[...] = jnp.concatenate(pieces, axis=1).astype(bf16)


def sb_attn_prompt(zt, sb_t, tq, tk):
    nseq, _, t = sb_t.shape
    nq = t // tq
    r = SB_G * tq
    assert tk % tq == 0 and t % tk == 0
    return pl.pallas_call(
        functools.partial(_sb_attn_kernel, tq=tq, tk=tk),
        grid=(nseq, nq),
        in_specs=[pl.BlockSpec((tq, H * HD), lambda b, i: (b * nq + i, ZT_SQ // (H * HD))),
                  pl.BlockSpec((None, 2 * SB_KVH * HD, t), lambda b, i: (b, 0, 0))],
        out_specs=pl.BlockSpec((tq, H * HD), lambda b, i: (b * nq + i, 0)),
        out_shape=jax.ShapeDtypeStruct((nseq * t, H * HD), bf16),
        scratch_shapes=[pltpu.VMEM((SB_KVH, r, 1), f32), pltpu.VMEM((SB_KVH, r, HD), f32)],
        compiler_params=_cparams("parallel", "parallel"),
        name="sb_attn",
    )(zt, sb_t)


def _select_blocks(score, blk, n_pick):
    sel = jnp.zeros_like(score)
    sc = score
    for _ in range(n_pick):
        mx = jnp.max(sc, axis=-1, keepdims=True)
        idx = jnp.min(jnp.where(sc == mx, blk, 1e9), axis=-1, keepdims=True)
        pick = blk == idx
        sel = jnp.where(pick & (mx >= 0.0), 1.0, sel)
        sc = jnp.where(pick, -3e38, sc)
    return sel


def _alibi_key_rows(kt, kpos, q0):
    n = kt.shape[1]
    rel = kpos - q0
    rowi = lax.broadcasted_iota(i32, (HD, n), 0)
    ext = jnp.where(rowi == 0, (rel >> 6).astype(f32), jnp.where(rowi == 1, (rel & 63).astype(f32), 0.0))
    return jnp.concatenate([kt, ext.astype(bf16)], axis=0)


def _ones_value_rows(vt):
    rowi = lax.broadcasted_iota(i32, vt.shape, 0)
    return jnp.concatenate([vt, jnp.where(rowi == 0, 1.0, 0.0).astype(bf16)], axis=0)


def _nsa_attn_kernel(q_ref, ng_ref, kv_ref, win_ref, pk_ref, pv_ref, o_ref, kc_ref, vc_ref, m_ref, acc_ref,
                     *, t, tq, tk):
    i = pl.program_id(1)
    n_cmp, n_sel = t // CMP_BLOCK, t // SEL_BLOCK
    r = H * tq
    q0 = i * tq

    @pl.when(i == 0)
    def _():
        kc_ref[...] = _dot(kv_ref[0:HD, :].astype(bf16), pk_ref[...])
        vc_ref[...] = _dot(kv_ref[HD:2 * HD, :].astype(bf16), pv_ref[...])

    slope = _slopes3()
    qs = _stack_heads(q_ref[...] * NSA_SCALE, H).astype(bf16)
    lane = lax.broadcasted_iota(i32, (1, 1, HD), 2)
    q_ext = jnp.where(lane == 0, slope * SEL_BLOCK, jnp.where(lane == 1, slope, 0.0))
    q_aug = jnp.concatenate([qs, jnp.broadcast_to(q_ext, (H, tq, HD)).reshape(r, HD).astype(bf16)], axis=1)
    qpos = q0 + lax.broadcasted_iota(i32, (1, tq, 1), 1)
    qf = qpos.astype(f32)

    cend = (lax.broadcasted_iota(i32, (1, 1, n_cmp), 2) + 1) * CMP_BLOCK - 1
    s = _dot(qs, kc_ref[...].astype(bf16)).reshape(H, tq, n_cmp) - slope * (qf - cend.astype(f32))
    p_c = _masked_softmax(s, cend <= qpos)
    o_c = _dot_nt(p_c.reshape(r, n_cmp).astype(bf16), vc_ref[...].astype(bf16)).reshape(H, tq, HD)

    imp_c = jnp.sum(p_c, axis=0)
    pair = (lax.broadcasted_iota(i32, (n_cmp, n_sel), 0) // (SEL_BLOCK // CMP_BLOCK)
            == lax.broadcasted_iota(i32, (n_cmp, n_sel), 1)).astype(bf16)
    imp = _split_dot(imp_c, pair, 3)
    blk_i = lax.broadcasted_iota(i32, (1, n_sel), 1)
    qp2 = q0 + lax.broadcasted_iota(i32, (tq, 1), 0)
    cur = qp2 // SEL_BLOCK
    forced = (blk_i == 0) | (blk_i == cur) | (blk_i == cur - 1)
    valid = blk_i * SEL_BLOCK <= qp2
    score = jnp.where(valid, imp + jnp.where(forced, FORCE_BONUS, 0.0), -1.0)
    sel = _select_blocks(score, blk_i.astype(f32), min(SEL_TOPN, n_sel)).astype(bf16)

    m_ref[...] = jnp.full_like(m_ref, NEG)
    acc_ref[...] = jnp.zeros_like(acc_ref)
    for c in range(t // tk):
        @pl.when(c * tk <= q0 + tq - 1)
        def _(c=c):
            kpos2 = c * tk + lax.broadcasted_iota(i32, (1, tk), 1)
            kt = _alibi_key_rows(kv_ref[2 * HD:3 * HD, c * tk:(c + 1) * tk].astype(bf16), kpos2, q0)
            vt = _ones_value_rows(kv_ref[3 * HD:4 * HD, c * tk:(c + 1) * tk].astype(bf16))
            expand = ((c * tk + lax.broadcasted_iota(i32, (n_sel, tk), 1)) // SEL_BLOCK
                      == lax.broadcasted_iota(i32, (n_sel, tk), 0)).astype(bf16)
            mask = ((_dot(sel, expand) > 0.5) & (kpos2 <= qp2))[None]
            sm = jnp.where(mask, _dot(q_aug, kt).reshape(H, tq, tk), NEG).reshape(r, tk)
            m_prev = m_ref[...]
            m_new = jnp.maximum(m_prev, jnp.max(sm, axis=-1, keepdims=True))
            p = jnp.exp(sm - m_new)
            acc_ref[...] = jnp.exp(m_prev - m_new) * acc_ref[...] + _dot_nt(p.astype(bf16), vt)
            m_ref[...] = m_new
    acc = acc_ref[...]
    l = acc[:, HD:HD + 1]
    o_s = (acc[:, 0:HD] / jnp.where(l > 0.0, l, 1.0)).reshape(H, tq, HD)

    nw = WINDOW + tq
    start = pl.multiple_of(jnp.maximum(q0 - WINDOW, 0), PAGE)
    kpos2 = start + lax.broadcasted_iota(i32, (1, nw), 1)
    kt = _alibi_key_rows(win_ref[0:HD, pl.ds(start, nw)].astype(bf16), kpos2, q0)
    vt = _ones_value_rows(win_ref[HD:2 * HD, pl.ds(start, nw)].astype(bf16))
    dist = qp2 - kpos2
    mask = ((dist >= 0) & (dist <= WINDOW))[None]
    sm = jnp.where(mask, _dot(q_aug, kt).reshape(H, tq, nw), NEG).reshape(r, nw)
    p = jnp.exp(sm - jnp.max(sm, axis=-1, keepdims=True))
    res = _dot_nt(p.astype(bf16), vt)
    o_w = (res[:, 0:HD] / res[:, HD:HD + 1]).reshape(H, tq, HD)

    gates = _sigmoid(ng_ref[...])
    col = lambda b: jnp.stack([gates[:, b * H + h:b * H + h + 1] for h in range(H)], axis=0)
    o = col(0) * o_c + col(1) * o_s + col(2) * o_w
    o_ref[...] = jnp.concatenate([o[h] for h in range(H)], axis=1).astype(bf16)


def nsa_attn_prompt(zt, nsa_t, win_t, pool_k, pool_v, tq, tk):
    nseq, _, t = nsa_t.shape
    nq = t // tq
    r = H * tq
    n_cmp = t // CMP_BLOCK
    assert tq % SEL_BLOCK == 0 and t % tk == 0
    const = lambda shape: pl.BlockSpec(shape, lambda b, i: (0,) * len(shape))
    return pl.pallas_call(
        functools.partial(_nsa_attn_kernel, t=t, tq=tq, tk=tk),
        grid=(nseq, nq),
        in_specs=[pl.BlockSpec((tq, H * HD), lambda b, i: (b * nq + i, ZT_NQ // (H * HD))),
                  pl.BlockSpec((tq, PAGE), lambda b, i: (b * nq + i, ZT_NG // PAGE)),
                  pl.BlockSpec((None, 4 * HD, t), lambda b, i: (b, 0, 0)),
                  pl.BlockSpec((None, 2 * HD, t), lambda b, i: (b, 0, 0)),
                  const((t, n_cmp)), const((t, n_cmp))],
        out_specs=pl.BlockSpec((tq, H * HD), lambda b, i: (b * nq + i, 0)),
        out_shape=jax.ShapeDtypeStruct((nseq * t, H * HD), bf16),
        scratch_shapes=[pltpu.VMEM((HD, n_cmp), f32), pltpu.VMEM((HD, n_cmp), f32),
                        pltpu.VMEM((r, 1), f32), pltpu.VMEM((r, 2 * HD), f32)],
        compiler_params=_cparams("parallel", "arbitrary"),
        name="nsa_attn",
    )(zt, zt, nsa_t, win_t, pool_k, pool_v)


def _merge_kernel(olat_ref, osb_ref, onsa_ref, gate_ref, x_ref, wuv_ref, pm_ref, ps_ref, pn_ref, wo_ref, o_ref):
    o_mla = jnp.concatenate([_dot_nt(olat_ref[h], wuv_ref[h]).astype(bf16) for h in range(H)], axis=1)
    gate = _sigmoid(gate_ref[...])
    m = (gate[:, 0:D_MODEL] * _dot(o_mla, pm_ref[...])
         + gate[:, D_MODEL:2 * D_MODEL] * _dot(osb_ref[...], ps_ref[...])
         + gate[:, 2 * D_MODEL:] * _dot(onsa_ref[...], pn_ref[...]))
    o_ref[...] = x_ref[...] + _dot(m.astype(bf16), wo_ref[...])


def merge(olat, osb, onsa, zt, x, wuv, pm, ps, pn, wo, tm):
    n = x.shape[0]
    const = lambda shape: pl.BlockSpec(shape, lambda i: (0,) * len(shape))
    row = lambda w: pl.BlockSpec((tm, w), lambda i: (i, 0))
    return pl.pallas_call(
        _merge_kernel,
        grid=(n // tm,),
        in_specs=[pl.BlockSpec((H, tm, KV_LORA), lambda i: (0, i, 0)), row(H * HD), row(H * HD), row(3 * D_MODEL),
                  row(D_MODEL), const((H, VDIM, KV_LORA)), const((H * VDIM, D_MODEL)), const((H * HD, D_MODEL)),
                  const((H * HD, D_MODEL)), const((D_MODEL, D_MODEL))],
        out_specs=row(D_MODEL),
        out_shape=jax.ShapeDtypeStruct((n, D_MODEL), f32),
        compiler_params=_cparams("parallel"),
        name="merge",
    )(olat, osb, onsa, zt, x, wuv, pm, ps, pn, wo)


FF_CHUNK = 256


def _gelu(x):
    return 0.5 * x * (1.0 + lax.erf(x * np.float32(np.sqrt(0.5))))


def _ffn_kernel(x_ref, g_ref, wup_ref, wval_ref, cw_ref, cb_ref, wdn_ref, gfin_ref, *rest, seq_mode, final_norm, nt):
    tm = x_ref.shape[0]
    if seq_mode:
        prev_ref, o_ref, st_ref, ubuf_ref, carry_ref = rest
    else:
        prev0_ref, prev1_ref, o_ref, st_ref = rest
    x = x_ref[...]
    hf = _rms_rows(x, g_ref[...]).astype(bf16)
    if seq_mode:
        @pl.when(pl.program_id(0) % nt == 0)
        def _():
            carry_ref[...] = prev_ref[...]
    acc = jnp.zeros((tm, D_MODEL), f32)
    for c in range(D_FF // FF_CHUNK):
        cols = slice(c * FF_CHUNK, (c + 1) * FF_CHUNK)
        u = _dot(hf, wup_ref[:, cols])
        v = _dot(hf, wval_ref[:, cols])
        if seq_mode:
            ubuf_ref[6:8, :] = carry_ref[:, cols]
            ubuf_ref[8:8 + tm, :] = u
            u2 = ubuf_ref[6:6 + tm, :]
            u1 = ubuf_ref[7:7 + tm, :]
            carry_ref[:, cols] = u[tm - 2:tm, :]
        else:
            u2 = prev0_ref[:, cols]
            u1 = prev1_ref[:, cols]
            st_ref[:, cols] = u
        uc = cb_ref[:, cols] + cw_ref[0:1, cols] * u2 + cw_ref[1:2, cols] * u1 + cw_ref[2:3, cols] * u
        acc = acc + _dot((_gelu(uc) * v).astype(bf16), wdn_ref[cols, :])
    if seq_mode:
        st_ref[...] = carry_ref[...]
    y = x + acc
    if final_norm:
        y = _rms_rows(y, gfin_ref[...])
    o_ref[...] = y


def ffn(x, g, wup, wval, cw, cb, wdn, gfin, prev, nseq, tm, final_norm):
    n = x.shape[0]
    seq_mode = not isinstance(prev, tuple)
    nt = (n // nseq) // tm if seq_mode else 1
    const = lambda shape: pl.BlockSpec(shape, lambda i: (0,) * len(shape))
    row = lambda w: pl.BlockSpec((tm, w), lambda i: (i, 0))
    in_specs = [row(D_MODEL), const((1, D_MODEL)), const((D_MODEL, D_FF)), const((D_MODEL, D_FF)), const((3, D_FF)),
                const((1, D_FF)), const((D_FF, D_MODEL)), const((1, D_MODEL))]
    if seq_mode:
        seq = pl.BlockSpec((None, 2, D_FF), lambda i: (i // nt, 0, 0))
        in_specs += [seq]
        out_specs = [row(D_MODEL), seq]
        out_shape = [jax.ShapeDtypeStruct((n, D_MODEL), f32), jax.ShapeDtypeStruct((nseq, 2, D_FF), f32)]
        scratch = [pltpu.VMEM((tm + 8, FF_CHUNK), f32), pltpu.VMEM((2, D_FF), f32)]
        args = (prev,)
    else:
        in_specs += [row(D_FF), row(D_FF)]
        out_specs = [row(D_MODEL), row(D_FF)]
        out_shape = [jax.ShapeDtypeStruct((n, D_MODEL), f32), jax.ShapeDtypeStruct((n, D_FF), f32)]
        scratch = []
        args = prev
    return pl.pallas_call(
        functools.partial(_ffn_kernel, seq_mode=seq_mode, final_norm=final_norm, nt=nt),
        grid=(n // tm,),
        in_specs=in_specs, out_specs=out_specs, out_shape=out_shape, scratch_shapes=scratch,
        compiler_params=_cparams("arbitrary"),
        name="ffn",
    )(x, g, wup, wval, cw, cb, wdn, gfin, *args)


def _page_specs(view_block, layer, npages, pages_per_step, order):
    def spec(p):
        def index(b, c, pt_ref):
            return (layer, pt_ref[b * npages + order(c) * pages_per_step + p]) + (0,) * (len(view_block) - 2)
        return pl.BlockSpec(view_block, index)
    return [spec(p) for p in range(pages_per_step)]


def _pages_per_step(npages, want):
    npg = min(want, npages)
    assert npages % npg == 0
    return npg


def _mla_dec_kernel(pt_ref, q_ref, new_ref, *refs, npg, nchunk):
    pages, (o_ref, m_ref, l_ref, acc_ref) = refs[:npg], refs[npg:]
    c = pl.program_id(1)

    @pl.when(c == 0)
    def _():
        m_ref[...] = jnp.full_like(m_ref, NEG)
        l_ref[...] = jnp.zeros_like(l_ref)
        acc_ref[...] = jnp.zeros_like(acc_ref)

    q = q_ref[0]
    kt = jnp.concatenate([pg[...].astype(bf16) for pg in pages], axis=1)
    s = _dot(q, kt)
    m_prev = m_ref[...]
    m_new = jnp.maximum(m_prev, jnp.max(s, axis=-1, keepdims=True))
    p = jnp.exp(s - m_new)
    alpha = jnp.exp(m_prev - m_new)
    l = alpha * l_ref[...] + jnp.sum(p, axis=-1, keepdims=True)
    acc = alpha * acc_ref[...] + _dot_nt(p.astype(bf16), kt[0:KV_LORA])
    m_ref[...] = m_new
    l_ref[...] = l
    acc_ref[...] = acc

    @pl.when(c == nchunk - 1)
    def _():
        knew = new_ref[0].astype(bf16).astype(f32)
        s_new = jnp.sum(q.astype(f32) * knew, axis=-1, keepdims=True)
        m_fin = jnp.maximum(m_new, s_new)
        p_new = jnp.exp(s_new - m_fin)
        a2 = jnp.exp(m_new - m_fin)
        l_fin = a2 * l + p_new
        acc_fin = a2 * acc + p_new.astype(bf16).astype(f32) * knew[:, 0:KV_LORA]
        o_ref[0] = acc_fin / l_fin


def mla_decode(cache_v, layer, pt, q, new_rows, want):
    bd = q.shape[0]
    npages = pt.shape[0] // bd
    npg = _pages_per_step(npages, want)
    nchunk = npages // npg
    specs = _page_specs((None, None, LAT, PAGE), layer, npages, npg, lambda c: c)
    return pl.pallas_call(
        functools.partial(_mla_dec_kernel, npg=npg, nchunk=nchunk),
        grid_spec=pltpu.PrefetchScalarGridSpec(
            num_scalar_prefetch=1, grid=(bd, nchunk),
            in_specs=[pl.BlockSpec((1, H, LAT), lambda b, c, p: (b, 0, 0)),
                      pl.BlockSpec((1, 1, LAT), lambda b, c, p: (b, 0, 0))] + specs,
            out_specs=pl.BlockSpec((1, H, KV_LORA), lambda b, c, p: (b, 0, 0)),
            scratch_shapes=[pltpu.VMEM((H, 1), f32), pltpu.VMEM((H, 1), f32), pltpu.VMEM((H, KV_LORA), f32)]),
        out_shape=jax.ShapeDtypeStruct((bd, H, KV_LORA), f32),
        compiler_params=_cparams("parallel", "arbitrary"),
        name="mla_dec",
    )(pt, q, new_rows, *([cache_v] * npg))


def _sb_dec_kernel(pt_ref, q_ref, *refs, npg, nchunk):
    pages, (o_ref, carry_ref, acc_ref) = refs[:npg], refs[npg:]
    c = pl.program_id(1)

    @pl.when(c == 0)
    def _():
        carry_ref[...] = jnp.zeros_like(carry_ref)
        acc_ref[...] = jnp.zeros_like(acc_ref)

    q = q_ref[0] * SB_SCALE
    rowq = lax.broadcasted_iota(i32, q.shape, 0)
    z = None
    for g in range(SB_KVH):
        q_g = jnp.where(rowq // SB_G == g, q, 0.0).astype(bf16)
        kt = jnp.concatenate([pg[0, g].astype(bf16) for pg in pages], axis=1)
        d = _dot(q_g, kt)
        z = d if z is None else z + d
    zp = jnp.concatenate([z[:, p * PAGE:(p + 1) * PAGE] for p in range(npg)], axis=0)
    lsp, lsn = _log_sigmoid_pair(zp)
    tri = (lax.broadcasted_iota(i32, (PAGE, PAGE), 0) > lax.broadcasted_iota(i32, (PAGE, PAGE), 1)).astype(bf16)
    suffix = _split_dot(lsn, tri, 2)
    tot = suffix[:, 0:1] + lsn[:, 0:1]
    carry = carry_ref[...]
    carries = [None] * npg
    for p in reversed(range(npg)):
        carries[p] = carry
        carry = carry + tot[p * H:(p + 1) * H]
    carry_ref[...] = carry
    a = jnp.exp(lsp + suffix + jnp.concatenate(carries, axis=0)).astype(bf16)
    aw = jnp.concatenate([a[p * H:(p + 1) * H] for p in range(npg)], axis=1)
    rowa = lax.broadcasted_iota(i32, aw.shape, 0)
    acc = acc_ref[...]
    for g in range(SB_KVH):
        vt = jnp.concatenate([pg[1, g].astype(bf16) for pg in pages], axis=1)
        acc = acc + _dot_nt(jnp.where(rowa // SB_G == g, aw, jnp.zeros_like(aw)), vt)
    acc_ref[...] = acc

    @pl.when(c == nchunk - 1)
    def _():
        o_ref[0] = acc


def sb_decode(cache_v, layer, pt, q, want):
    bd = q.shape[0]
    npages = pt.shape[0] // bd
    npg = _pages_per_step(npages, want)
    nchunk = npages // npg
    specs = _page_specs((None, None, 2, SB_KVH, HD, PAGE), layer, npages, npg, lambda c: nchunk - 1 - c)
    return pl.pallas_call(
        functools.partial(_sb_dec_kernel, npg=npg, nchunk=nchunk),
        grid_spec=pltpu.PrefetchScalarGridSpec(
            num_scalar_prefetch=1, grid=(bd, nchunk),
            in_specs=[pl.BlockSpec((1, H, HD), lambda b, c, p: (b, 0, 0))] + specs,
            out_specs=pl.BlockSpec((1, H, HD), lambda b, c, p: (b, 0, 0)),
            scratch_shapes=[pltpu.VMEM((H, 1), f32), pltpu.VMEM((H, HD), f32)]),
        out_shape=jax.ShapeDtypeStruct((bd, H, HD), f32),
        compiler_params=_cparams("parallel", "arbitrary"),
        name="sb_dec",
    )(pt, q, *([cache_v] * npg))


def _nsa_cmp_dec_kernel(pt_ref, q_ref, pk_ref, pv_ref, *refs, npg, nchunk, n_cmp):
    pages, (oc_ref, idx_ref, kc_ref, vc_ref) = refs[:npg], refs[npg:]
    c = pl.program_id(1)
    w = npg * PAGE // CMP_BLOCK
    ck = jnp.concatenate([pg[0].astype(bf16) for pg in pages], axis=1)
    cv = jnp.concatenate([pg[1].astype(bf16) for pg in pages], axis=1)
    kcb = _dot(ck, pk_ref[...])
    vcb = _dot(cv, pv_ref[...])
    for k in range(nchunk):
        @pl.when(c == k)
        def _(k=k):
            kc_ref[:, k * w:(k + 1) * w] = kcb
            vc_ref[:, k * w:(k + 1) * w] = vcb

    @pl.when(c == nchunk - 1)
    def _():
        q = (q_ref[0] * NSA_SCALE).astype(bf16)
        n_sel = n_cmp // 2 + 1
        qposf = jnp.float32(n_cmp * CMP_BLOCK)
        cend = ((lax.broadcasted_iota(i32, (1, n_cmp), 1) + 1) * CMP_BLOCK - 1).astype(f32)
        slope = _slopes3().reshape(H, 1)
        s = _dot(q, kc_ref[...].astype(bf16)) - slope * (qposf - cend)
        m = jnp.max(s, axis=-1, keepdims=True)
        p = jnp.exp(s - m)
        p = p / jnp.sum(p, axis=-1, keepdims=True)
        oc_ref[0] = _dot_nt(p.astype(bf16), vc_ref[...].astype(bf16))
        imp_c = jnp.sum(p, axis=0, keepdims=True)
        nl = idx_ref.shape[-1]
        pair = (lax.broadcasted_iota(i32, (n_cmp, nl), 0) // 2 == lax.broadcasted_iota(i32, (n_cmp, nl), 1)).astype(bf16)
        imp = _split_dot(jnp.broadcast_to(imp_c, (H, n_cmp)), pair, 3)[0:1]
        blk = lax.broadcasted_iota(i32, (1, nl), 1)
        cur = n_sel - 1
        forced = (blk == 0) | (blk == cur) | (blk == cur - 1)
        score = jnp.where(blk < n_sel, imp + jnp.where(forced, FORCE_BONUS, 0.0), -1.0)
        blkf = blk.astype(f32)
        out = jnp.full((1, nl), -1.0, f32)
        sc = score
        for k in range(min(SEL_TOPN, n_sel)):
            mx = jnp.max(sc, axis=-1, keepdims=True)
            idx = jnp.min(jnp.where(sc == mx, blkf, 1e9), axis=-1, keepdims=True)
            out = jnp.where((blk == k) & (mx >= 0.0), idx, out)
            sc = jnp.where(blkf == idx, -3e38, sc)
        idx_ref[0] = out.astype(i32)


def nsa_cmp_decode(cache_v, layer, pt, q, pool_k, pool_v, npg):
    bd = q.shape[0]
    npages = pt.shape[0] // bd
    nchunk = npages // npg
    n_cmp = npages * PAGE // CMP_BLOCK
    w = npg * PAGE // CMP_BLOCK
    nl = max(PAGE, -(-(n_cmp // 2 + 1) // PAGE) * PAGE)
    specs = _page_specs((None, None, 2, HD, PAGE), layer, npages, npg, lambda c: c)
    pool_spec = pl.BlockSpec((npg * PAGE, w), lambda b, c, p: (0, 0))
    return pl.pallas_call(
        functools.partial(_nsa_cmp_dec_kernel, npg=npg, nchunk=nchunk, n_cmp=n_cmp),
        grid_spec=pltpu.PrefetchScalarGridSpec(
            num_scalar_prefetch=1, grid=(bd, nchunk),
            in_specs=[pl.BlockSpec((1, H, HD), lambda b, c, p: (b, 0, 0)), pool_spec, pool_spec] + specs,
            out_specs=[pl.BlockSpec((1, H, HD), lambda b, c, p: (b, 0, 0)),
                       pl.BlockSpec((1, 1, nl), lambda b, c, p: (b, 0, 0))],
            scratch_shapes=[pltpu.VMEM((HD, n_cmp), f32), pltpu.VMEM((HD, n_cmp), f32)]),
        out_shape=[jax.ShapeDtypeStruct((bd, H, HD), f32), jax.ShapeDtypeStruct((bd, 1, nl), i32)],
        compiler_params=_cparams("parallel", "arbitrary"),
        name="nsa_cmp_dec",
    )(pt, q, pool_k, pool_v, *([cache_v] * npg))


def _nsa_sel_dec_kernel(phys_ref, blk_ref, q_ref, new_ref, oc_ref, ng_ref, win_ref, *refs, nsl, past):
    pages, (o_ref,) = refs[:nsl], refs[nsl:]
    b = pl.program_id(0)
    qf32 = q_ref[0] * NSA_SCALE
    q = qf32.astype(bf16)
    qf32 = q.astype(f32)
    slope = _slopes3().reshape(H, 1)
    new = new_ref[0].astype(bf16).astype(f32)
    lane = lax.broadcasted_iota(i32, (1, PAGE), 1)

    ss, masks = [], []
    for j in range(nsl):
        blk = blk_ref[b * nsl + j]
        tok = (blk // 2) * PAGE + lane
        valid = (blk >= 0) & (blk * SEL_BLOCK < past) & (lane // SEL_BLOCK == blk % 2)
        s = _dot(q, pages[j][0].astype(bf16)) - slope * (past - tok).astype(f32)
        ss.append(jnp.where(valid, s, NEG))
        masks.append(valid)
    s_new = jnp.sum(qf32 * new[2:3], axis=-1, keepdims=True)
    m = jnp.maximum(functools.reduce(jnp.maximum, [jnp.max(s, axis=-1, keepdims=True) for s in ss]), s_new)
    p_new = jnp.exp(s_new - m)
    l = p_new
    acc = p_new.astype(bf16).astype(f32) * new[3:4]
    for j in range(nsl):
        p = jnp.where(masks[j], jnp.exp(ss[j] - m), 0.0)
        l = l + jnp.sum(p, axis=-1, keepdims=True)
        acc = acc + _dot_nt(p.astype(bf16), pages[j][1].astype(bf16))
    o_s = acc / l

    nwin = win_ref.shape[-1]
    dist = (nwin - lax.broadcasted_iota(i32, (1, nwin), 1)).astype(f32)
    s = _dot(q, win_ref[0].astype(bf16)) - slope * dist
    s_new = jnp.sum(qf32 * new[4:5], axis=-1, keepdims=True)
    m = jnp.maximum(jnp.max(s, axis=-1, keepdims=True), s_new)
    p = jnp.exp(s - m)
    p_new = jnp.exp(s_new - m)
    l = jnp.sum(p, axis=-1, keepdims=True) + p_new
    o_w = (_dot_nt(p.astype(bf16), win_ref[1].astype(bf16)) + p_new.astype(bf16).astype(f32) * new[5:6]) / l

    gates = _sigmoid(ng_ref[0])
    o_ref[0] = gates[0] * oc_ref[0] + gates[1] * o_s + gates[2] * o_w


def nsa_sel_decode(cache_v, win_v, layer, phys, blks, q, new6, o_c, ng3, past):
    bd = q.shape[0]
    nsl = phys.shape[0] // bd
    nwin = win_v.shape[-1]

    def page_spec(j):
        return pl.BlockSpec((None, None, 2, HD, PAGE), lambda b, ph, bl: (layer, ph[b * nsl + j], 1, 0, 0))

    per_seq = lambda shape: pl.BlockSpec((1,) + shape, lambda b, ph, bl: (b,) + (0,) * len(shape))
    return pl.pallas_call(
        functools.partial(_nsa_sel_dec_kernel, nsl=nsl, past=past),
        grid_spec=pltpu.PrefetchScalarGridSpec(
            num_scalar_prefetch=2, grid=(bd,),
            in_specs=[per_seq((H, HD)), per_seq((6, HD)), per_seq((H, HD)), per_seq((3, H, 1)),
                      pl.BlockSpec((None, None, 2, HD, nwin), lambda b, ph, bl: (layer, b, 0, 0, 0))]
                     + [page_spec(j) for j in range(nsl)],
            out_specs=per_seq((H, HD))),
        out_shape=jax.ShapeDtypeStruct((bd, H, HD), f32),
        compiler_params=_cparams("parallel"),
        name="nsa_sel_dec",
    )(phys, blks, q, new6, o_c, ng3, win_v, *([cache_v] * nsl))


def _pool_matrix(w, ntok):
    tok = np.arange(ntok)
    onehot = jnp.asarray(tok[:, None] // CMP_BLOCK == np.arange(ntok // CMP_BLOCK)[None, :], f32)
    return (onehot[None] * jnp.tile(w, (1, ntok // CMP_BLOCK))[:, :, None]).astype(bf16)


def _prep_params(w_in, w_uq, w_uk, w_uv, pool_k, pool_v, p_mla, p_sb, p_nsa, w_o, w_up, w_val, w_down, t_prompt, n_dec):
    depth = w_in.shape[0]
    off = np.concatenate([[0], np.cumsum(IN_WIDTHS)])
    wt = jnp.swapaxes(w_in, 1, 2)
    seg = lambda k: wt[:, off[k]:off[k + 1]]
    cq, ckv, kr, sq, sk, sv, nq, nkv, ng, mg = [seg(k) for k in range(10)]
    half = ROPE // 2
    kr_sw = jnp.concatenate([-kr[:, half:], kr[:, :half]], axis=1)
    zpad = jnp.zeros((depth, ZT_W - ZT_NG - 3 * H, D_MODEL), f32)
    w_tok = jnp.concatenate([mg, nq, sq, cq, ng, zpad], axis=1).astype(bf16)
    w_feat = jnp.concatenate([sk, sv, nkv, ckv, kr, kr_sw], axis=1).astype(bf16)
    uq = w_uq.reshape(depth, Q_LORA, H, NOPE + ROPE)
    uq_r = uq[..., NOPE:]
    uq_sw = jnp.concatenate([-uq_r[..., half:], uq_r[..., :half]], axis=-1)
    wuq = jnp.concatenate([uq[..., :NOPE].reshape(depth, Q_LORA, -1), uq_r.reshape(depth, Q_LORA, -1),
                           uq_sw.reshape(depth, Q_LORA, -1)], axis=-1).astype(bf16)
    wuk = jnp.transpose(w_uk, (0, 2, 3, 1)).astype(bf16)
    wuv = jnp.transpose(w_uv, (0, 2, 3, 1)).astype(bf16)
    cast = lambda a: a.astype(bf16)
    return dict(w_tok=w_tok, w_feat=w_feat, wuq=wuq, wuk=wuk, wuv=wuv,
                pool_pk=_pool_matrix(pool_k, t_prompt), pool_pv=_pool_matrix(pool_v, t_prompt),
                pool_dk=_pool_matrix(pool_k, n_dec), pool_dv=_pool_matrix(pool_v, n_dec),
                p_mla=cast(p_mla), p_sb=cast(p_sb), p_nsa=cast(p_nsa), w_o=cast(w_o), w_up=cast(w_up),
                w_val=cast(w_val), w_down=cast(w_down))


def _rope_tables(pos):
    half = ROPE // 2
    inv = ROPE_BASE ** (-jnp.arange(half, dtype=f32) / half)
    ang = pos.astype(f32)[:, None] * inv[None, :]
    cos = jnp.concatenate([jnp.cos(ang)] * 2, axis=1)
    sin = jnp.concatenate([jnp.sin(ang)] * 2, axis=1)
    return jnp.tile(cos, (1, H)), jnp.tile(sin, (1, H)), cos.T, sin.T


def kernel(x_prompt, x_sample, cache_mla, cache_sb, cache_nsa, state_nsa_win, state_ffn_conv, page_table, g_attn, w_in, g_q, w_uq, g_kv, w_uk, w_uv, nsa_pool_k, nsa_pool_v, p_mla, p_sb, p_nsa, w_o, g_ffn, w_up, w_val, conv_w, conv_b, w_down, g_final):
    depth = w_in.shape[0]
    bp, t, _ = x_prompt.shape
    bd = x_sample.shape[0]
    npages = page_table.shape[1]
    past = npages * PAGE
    nwin = state_nsa_win.shape[2]
    assert x_sample.shape[1] == 1 and t % 512 == 0 and t >= WINDOW + 128 and bd % 8 == 0
    assert nwin == WINDOW and past >= WINDOW
    npg_cmp = _pages_per_step(npages, 32)
    assert npages == npg_cmp or (npg_cmp * PAGE // CMP_BLOCK) % PAGE == 0

    w = _prep_params(w_in, w_uq, w_uk, w_uv, nsa_pool_k, nsa_pool_v, p_mla, p_sb, p_nsa, w_o, w_up, w_val, w_down,
                     t, npg_cmp * PAGE)
    cos8_p, sin8_p, cost_p, sint_p = _rope_tables(jnp.arange(t))
    cos8_d, sin8_d, cost_d, sint_d = _rope_tables(jnp.full((bd,), past))
    mla_v = jnp.swapaxes(cache_mla, 2, 3)
    sb_v = jnp.transpose(cache_sb, (0, 1, 3, 4, 5, 2))
    nsa_v = jnp.transpose(cache_nsa, (0, 1, 3, 4, 5, 2)).reshape(depth, -1, 4, HD, PAGE)
    win_v = jnp.transpose(state_nsa_win, (0, 1, 3, 4, 5, 2)).reshape(depth, bd, 2, HD, nwin)
    pt = page_table.reshape(-1)
    row2 = lambda a: a.reshape(depth, 1, -1)
    gfin = g_final.reshape(1, -1)

    hp = x_prompt.reshape(bp * t, D_MODEL)
    hs = x_sample.reshape(bd, D_MODEL)
    st_p = [[] for _ in range(5)]
    st_s = [[] for _ in range(5)]
    zero_prev = jnp.zeros((bp, 2, D_FF), f32)
    for l in range(depth):
        last = l == depth - 1
        ga, gq, gf, cb = row2(g_attn)[l], row2(g_q)[l], row2(g_ffn)[l], row2(conv_b)[l]
        gkv = g_kv[l].reshape(-1, 1)
        cw = conv_w[l]

        zt, sb_t, nsa_t, win_t, mla_t = in_proj(hp, ga, w["w_tok"][l], w["w_feat"][l], bp, 256)
        q, rows_t, kt = mla_prep(zt, mla_t, cos8_p, sin8_p, cost_p, sint_p, gq, w["wuq"][l], gkv, w["wuk"][l], 256)
        o_lat = mla_attn_prompt(q, kt, 256, 512)
        o_sb = sb_attn_prompt(zt, sb_t, 128, 256)
        o_nsa = nsa_attn_prompt(zt, nsa_t, win_t, w["pool_pk"][l], w["pool_pv"][l], 128, 512)
        hp = merge(o_lat, o_sb, o_nsa, zt, hp, w["wuv"][l], w["p_mla"][l], w["p_sb"][l], w["p_nsa"][l], w["w_o"][l], 256)
        hp, conv_p = ffn(hp, gf, w["w_up"][l], w["w_val"][l], cw, cb, w["w_down"][l], gfin, zero_prev, bp, 256, last)
        for k, a in enumerate((rows_t, sb_t, nsa_t, win_t[:, :, t - WINDOW:], conv_p)):
            st_p[k].append(a)

        zt, sb_n, nsa_n, win_n, mla_t = in_proj(hs, ga, w["w_tok"][l], w["w_feat"][l], 1, bd)
        q, rows_n, _ = mla_prep(zt, mla_t, cos8_d, sin8_d, cost_d, sint_d, gq, w["wuq"][l], gkv, w["wuk"][l], bd)
        rows_n, sb_n, nsa_n, win_n = (jnp.swapaxes(a[0], 0, 1) for a in (rows_n, sb_n, nsa_n, win_n))
        o_lat = mla_decode(mla_v, l, pt, jnp.swapaxes(q, 0, 1), rows_n.reshape(bd, 1, LAT), 32)
        o_lat = jnp.swapaxes(o_lat, 0, 1).astype(bf16)
        q_sb = zt[:, ZT_SQ:ZT_SQ + H * HD].reshape(bd, H, HD)
        o_sb = sb_decode(sb_v, l, pt, q_sb, 16).reshape(bd, H * HD).astype(bf16)
        q_nsa = zt[:, ZT_NQ:ZT_NQ + H * HD].reshape(bd, H, HD)
        o_c, idx = nsa_cmp_decode(nsa_v, l, pt, q_nsa, w["pool_dk"][l], w["pool_dv"][l], npg_cmp)
        blks = idx[:, 0, :SEL_TOPN]
        phys = jnp.take_along_axis(page_table, jnp.clip(blks // 2, 0, npages - 1), axis=1)
        new6 = jnp.concatenate([nsa_n, win_n], axis=1).reshape(bd, 6, HD)
        ng3 = zt[:, ZT_NG:ZT_NG + 3 * H].reshape(bd, 3, H, 1)
        o_nsa = nsa_sel_decode(nsa_v, win_v, l, phys.reshape(-1), blks.reshape(-1), q_nsa, new6, o_c, ng3, past)
        o_nsa = o_nsa.reshape(bd, H * HD).astype(bf16)
        hs = merge(o_lat, o_sb, o_nsa, zt, hs, w["wuv"][l], w["p_mla"][l], w["p_sb"][l], w["p_nsa"][l], w["w_o"][l], bd)
        prev = (state_ffn_conv[l, :, 0], state_ffn_conv[l, :, 1])
        hs, u_new = ffn(hs, gf, w["w_up"][l], w["w_val"][l], cw, cb, w["w_down"][l], gfin, prev, bd, bd, last)
        win_s = jnp.concatenate([win_v[l][..., 1:], win_n.reshape(bd, 2, HD, 1)], axis=-1)
        for k, a in enumerate((rows_n, sb_n, nsa_n, win_s, jnp.stack([prev[1], u_new], axis=1))):
            st_s[k].append(a)

    stack = lambda xs: jnp.stack(xs)
    fm_to_tm = lambda a, shape: jnp.moveaxis(a, -1, 2).reshape(shape)
    y_prompt = hp.reshape(bp, t, D_MODEL)
    y_sample = hs.reshape(bd, 1, D_MODEL)
    new_mla_p = fm_to_tm(stack(st_p[0]), (depth, bp, t, LAT))
    new_sb_p = fm_to_tm(stack(st_p[1]), (depth, bp, t, 2, SB_KVH, HD))
    new_nsa_p = fm_to_tm(stack(st_p[2]), (depth, bp, t, 4, 1, HD))
    new_win_p = fm_to_tm(stack(st_p[3]), (depth, bp, WINDOW, 2, 1, HD))
    new_conv_p = stack(st_p[4])
    new_mla_s = stack(st_s[0]).reshape(depth, bd, 1, LAT)
    new_sb_s = stack(st_s[1]).reshape(depth, bd, 1, 2, SB_KVH, HD)
    new_nsa_s = stack(st_s[2]).reshape(depth, bd, 1, 4, 1, HD)
    new_win_s = jnp.moveaxis(stack(st_s[3]), -1, 2).reshape(depth, bd, WINDOW, 2, 1, HD)
    new_conv_s = stack(st_s[4])
    return (y_prompt, y_sample, new_mla_p, new_sb_p, new_nsa_p, new_win_p, new_conv_p,
            new_mla_s, new_sb_s, new_nsa_s, new_win_s, new_conv_s)
```

```python
import functools

import numpy as np
import jax
import jax.numpy as jnp
from jax import lax
from jax.experimental import pallas as pl
from jax.experimental.pallas import tpu as pltpu

f32 = jnp.float32
bf16 = jnp.bfloat16
i32 = jnp.int32

D_MODEL = 1024
PAGE = 128
H = 8
Q_LORA, KV_LORA, NOPE, ROPE, VDIM = 256, 128, 64, 32, 64
LAT = KV_LORA + ROPE
KT_ROWS = LAT + 8
ROPE_BASE = 10000.0
MLA_SCALE = (NOPE + ROPE) ** -0.5
SB_KVH, SB_G, HD = 2, 4, 64
SB_SCALE = HD ** -0.5
NSA_SCALE = HD ** -0.5
CMP_BLOCK, SEL_BLOCK, SEL_TOPN, WINDOW = 32, 64, 8, 512
FORCE_BONUS = 1.0e4
D_FF = 2816
NORM_EPS = 1e-6
NEG = -1e30
IN_WIDTHS = (Q_LORA, KV_LORA, ROPE, H * HD, SB_KVH * HD, SB_KVH * HD, H * HD, 6 * HD, 3 * H, 3 * D_MODEL)

ZT_GATE, ZT_NQ, ZT_SQ, ZT_CQ, ZT_NG, ZT_W = 0, 3072, 3584, 4096, 4352, 4480
ZF_SB, ZF_NSA, ZF_WIN, ZF_MLA, ZF_W = 0, 256, 512, 640, 832

VMEM_LIMIT = 56 * 1024 * 1024


def _cparams(*sem):
    return pltpu.CompilerParams(dimension_semantics=sem, vmem_limit_bytes=VMEM_LIMIT)


def _dot(a, b):
    return jnp.dot(a, b, preferred_element_type=f32)


def _dot_nt(a, b):
    return lax.dot_general(a, b, (((1,), (1,)), ((), ())), preferred_element_type=f32)


def _split_dot(a, b, terms):
    out = None
    r = a
    for t in range(terms):
        piece = r.astype(bf16)
        d = _dot(piece, b)
        out = d if out is None else out + d
        if t + 1 < terms:
            r = r - piece.astype(f32)
    return out


def _rms_rows(x, g):
    return x * lax.rsqrt(jnp.mean(x * x, axis=-1, keepdims=True) + NORM_EPS) * g


def _sigmoid(x):
    return 1.0 / (1.0 + jnp.exp(-x))


def _slopes3():
    h = lax.broadcasted_iota(i32, (H, 1, 1), 0)
    s = jnp.full((H, 1, 1), 2.0 ** -H, f32)
    for k in range(H - 1):
        s = jnp.where(h == k, 2.0 ** -(k + 1), s)
    return s


def _stack_heads(q, n):
    return jnp.concatenate([q[:, k * HD:(k + 1) * HD] for k in range(n)], axis=0)


def _row_max(s):
    n = s.shape[-1]
    if n % PAGE == 0 and n > PAGE:
        tiles = [s[..., k * PAGE:(k + 1) * PAGE] for k in range(n // PAGE)]
        s = functools.reduce(jnp.maximum, tiles)
    return jnp.max(s, axis=-1, keepdims=True)


def _masked_softmax(s, mask):
    sm = jnp.where(mask, s, NEG)
    m = jnp.max(sm, axis=-1, keepdims=True)
    p = jnp.where(mask, jnp.exp(sm - m), 0.0)
    l = jnp.sum(p, axis=-1, keepdims=True)
    return p / jnp.where(l > 0.0, l, 1.0)


def _in_proj_kernel(x_ref, g_ref, wt_ref, wf_ref, zt_ref, sb_ref, nsa_ref, win_ref, mla_ref):
    hn = _rms_rows(x_ref[...], g_ref[...]).astype(bf16)
    zt_ref[...] = _dot_nt(hn, wt_ref[...])
    zf = _dot_nt(wf_ref[...], hn)
    sb_ref[...] = zf[ZF_SB:ZF_NSA]
    nsa_ref[...] = zf[ZF_NSA:ZF_WIN]
    win_ref[...] = zf[ZF_WIN:ZF_MLA]
    mla_ref[...] = zf[ZF_MLA:ZF_W]


def in_proj(x, g, wt, wf, nseq, tm):
    n = x.shape[0]
    t = n // nseq
    nt = t // tm
    fm = lambda rows: pl.BlockSpec((None, rows, tm), lambda i: (i // nt, 0, i % nt))
    const = lambda shape: pl.BlockSpec(shape, lambda i: (0,) * len(shape))
    return pl.pallas_call(
        _in_proj_kernel,
        grid=(n // tm,),
        in_specs=[pl.BlockSpec((tm, D_MODEL), lambda i: (i, 0)), const((1, D_MODEL)),
                  const((ZT_W, D_MODEL)), const((ZF_W, D_MODEL))],
        out_specs=[pl.BlockSpec((tm, ZT_W), lambda i: (i, 0)), fm(256), fm(256), fm(128), fm(192)],
        out_shape=[jax.ShapeDtypeStruct((n, ZT_W), f32),
                   jax.ShapeDtypeStruct((nseq, 256, t), f32), jax.ShapeDtypeStruct((nseq, 256, t), f32),
                   jax.ShapeDtypeStruct((nseq, 128, t), f32), jax.ShapeDtypeStruct((nseq, 192, t), f32)],
        compiler_params=_cparams("parallel"),
        name="in_proj",
    )(x, g, wt, wf)


def _mla_prep_kernel(cq_ref, cos_ref, sin_ref, mla_ref, cost_ref, sint_ref, gq_ref, wuq_ref, gkv_ref, wuk_ref,
                     q_ref, rows_ref, kt_ref):
    cqn = _rms_rows(cq_ref[...], gq_ref[...]).astype(bf16)
    qa = _dot(cqn, wuq_ref[...])
    nr = H * NOPE
    qr = (qa[:, nr:nr + H * ROPE] * cos_ref[...] + qa[:, nr + H * ROPE:] * sin_ref[...]) * MLA_SCALE
    for h in range(H):
        ql = _dot(qa[:, h * NOPE:(h + 1) * NOPE].astype(bf16), wuk_ref[h]) * MLA_SCALE
        q_ref[h, :, 0:KV_LORA] = ql.astype(bf16)
        q_ref[h, :, KV_LORA:LAT] = qr[:, h * ROPE:(h + 1) * ROPE].astype(bf16)
    m = mla_ref[...]
    ckv = m[0:KV_LORA]
    ckn = ckv * lax.rsqrt(jnp.mean(ckv * ckv, axis=0, keepdims=True) + NORM_EPS) * gkv_ref[...]
    kr = m[KV_LORA:LAT] * cost_ref[...] + m[LAT:LAT + ROPE] * sint_ref[...]
    rows_ref[0:KV_LORA, :] = ckn
    rows_ref[KV_LORA:LAT, :] = kr
    kt_ref[0:KV_LORA, :] = ckn.astype(bf16)
    kt_ref[KV_LORA:LAT, :] = kr.astype(bf16)
    tail = kt_ref.shape[0] - LAT
    rowi = lax.broadcasted_iota(i32, (tail, kt_ref.shape[1]), 0)
    kt_ref[LAT:, :] = jnp.where(rowi == 0, 1.0, 0.0).astype(bf16)


def mla_prep(zt, mla_t, cos8, sin8, cos_t, sin_t, gq, wuq, gkv, wuk, tm):
    n = zt.shape[0]
    nseq, _, t = mla_t.shape
    nt = t // tm
    const = lambda shape: pl.BlockSpec(shape, lambda i: (0,) * len(shape))
    fm = lambda rows: pl.BlockSpec((None, rows, tm), lambda i: (i // nt, 0, i % nt))
    return pl.pallas_call(
        _mla_prep_kernel,
        grid=(n // tm,),
        in_specs=[pl.BlockSpec((tm, Q_LORA), lambda i: (i, ZT_CQ // Q_LORA)),
                  pl.BlockSpec((tm, H * ROPE), lambda i: (i % nt, 0)), pl.BlockSpec((tm, H * ROPE), lambda i: (i % nt, 0)),
                  fm(192),
                  pl.BlockSpec((ROPE, tm), lambda i: (0, i % nt)), pl.BlockSpec((ROPE, tm), lambda i: (0, i % nt)),
                  const((1, Q_LORA)), const((Q_LORA, H * (NOPE + 2 * ROPE))), const((KV_LORA, 1)),
                  const((H, NOPE, KV_LORA))],
        out_specs=[pl.BlockSpec((H, tm, LAT), lambda i: (0, i, 0)), fm(LAT), fm(KT_ROWS)],
        out_shape=[jax.ShapeDtypeStruct((H, n, LAT), bf16), jax.ShapeDtypeStruct((nseq, LAT, t), f32),
                   jax.ShapeDtypeStruct((nseq, KT_ROWS, t), bf16)],
        compiler_params=_cparams("parallel"),
        name="mla_prep",
    )(zt, cos8, sin8, mla_t, cos_t, sin_t, gq, wuq, gkv, wuk)


MLA_HEADS_PER_PASS = 2


def _mla_attn_kernel(q_ref, kt_ref, o_ref, m_ref, acc_ref, *, tq, tk):
    i = pl.program_id(1)
    r = H * tq
    m_ref[...] = jnp.full_like(m_ref, NEG)
    acc_ref[...] = jnp.zeros_like(acc_ref)
    q = q_ref[...].reshape(r, LAT)

    hs = MLA_HEADS_PER_PASS
    rs = hs * tq

    def update(off, width, masked):
        kt = kt_ref[:, pl.ds(off, width)]
        if masked:
            qpos = i * tq + lax.broadcasted_iota(i32, (1, tq, 1), 1)
            kpos = off + lax.broadcasted_iota(i32, (1, 1, width), 2)
            visible = kpos <= qpos
        for c in range(H // hs):
            rows = slice(c * rs, (c + 1) * rs)
            s = _dot(q[rows], kt[0:LAT])
            if masked:
                s = jnp.where(visible, s.reshape(hs, tq, width), NEG).reshape(rs, width)
            m_prev = m_ref[rows]
            m_new = jnp.maximum(m_prev, _row_max(s))
            p = jnp.exp(s - m_new)
            acc_ref[rows] = jnp.exp(m_prev - m_new) * acc_ref[rows] + _dot_nt(p.astype(bf16), kt)
            m_ref[rows] = m_new

    nfull = (i * tq) // tk

    def body(j, carry):
        update(pl.multiple_of(j * tk, tk), tk, False)
        return carry

    lax.fori_loop(0, nfull, body, 0)
    update(pl.multiple_of(nfull * tk, tk), tk, True)
    acc = acc_ref[...]
    o_ref[...] = (acc[:, 0:KV_LORA] / acc[:, LAT:LAT + 1]).reshape(H, tq, KV_LORA).astype(bf16)


def mla_attn_prompt(q, kt, tq, tk):
    nseq, _, t = kt.shape
    nq = t // tq
    r = H * tq
    assert tk % tq == 0 and t % tk == 0
    return pl.pallas_call(
        functools.partial(_mla_attn_kernel, tq=tq, tk=tk),
        grid=(nseq, nq),
        in_specs=[pl.BlockSpec((H, tq, LAT), lambda b, i: (0, b * nq + i, 0)),
                  pl.BlockSpec((None, KT_ROWS, t), lambda b, i: (b, 0, 0))],
        out_specs=pl.BlockSpec((H, tq, KV_LORA), lambda b, i: (0, b * nq + i, 0)),
        out_shape=jax.ShapeDtypeStruct((H, nseq * t, KV_LORA), bf16),
        scratch_shapes=[pltpu.VMEM((r, 1), f32), pltpu.VMEM((r, KT_ROWS), f32)],
        compiler_params=_cparams("parallel", "parallel"),
        name="mla_attn",
    )(q, kt)


def _log_sigmoid_pair(z):
    t = jnp.log(1.0 + jnp.exp(-jnp.abs(z)))
    lsp = jnp.minimum(z, 0.0) - t
    return lsp, lsp - z


SB_HEADS_PER_PASS = 4


def _sb_attn_kernel(q_ref, kv_ref, o_ref, carry_ref, acc_ref, *, tq, tk):
    i = pl.program_id(1)
    r = SB_G * tq
    hs = SB_HEADS_PER_PASS
    rs = hs * tq
    q = q_ref[...] * SB_SCALE
    qs = [_stack_heads(q[:, g * SB_G * HD:(g + 1) * SB_G * HD], SB_G).astype(bf16) for g in range(SB_KVH)]
    carry_ref[...] = jnp.zeros_like(carry_ref)
    acc_ref[...] = jnp.zeros_like(acc_ref)
    tri = (lax.broadcasted_iota(i32, (tk, tk), 0) > lax.broadcasted_iota(i32, (tk, tk), 1)).astype(bf16)

    def block(jj, masked):
        cols = pl.ds(pl.multiple_of(jj * tk, tk), tk)
        if masked:
            qpos = i * tq + lax.broadcasted_iota(i32, (1, tq, 1), 1)
            kpos = jj * tk + lax.broadcasted_iota(i32, (1, 1, tk), 2)
            mask = kpos < qpos
        for g in range(SB_KVH):
            kt = kv_ref[g * HD:(g + 1) * HD, cols].astype(bf16)
            vt = kv_ref[(SB_KVH + g) * HD:(SB_KVH + g + 1) * HD, cols].astype(bf16)
            for sp in range(SB_G // hs):
                rows = slice(sp * rs, (sp + 1) * rs)
                lsp, lsn = _log_sigmoid_pair(_dot(qs[g][rows], kt))
                if masked:
                    lsn = jnp.where(mask, lsn.reshape(hs, tq, tk), 0.0).reshape(rs, tk)
                suffix = _split_dot(lsn, tri, 2)
                carry = carry_ref[g, rows]
                a = jnp.exp(lsp + suffix + carry)
                if masked:
                    a = jnp.where(mask, a.reshape(hs, tq, tk), 0.0).reshape(rs, tk)
                acc_ref[g, rows] = acc_ref[g, rows] + _dot_nt(a.astype(bf16), vt)
                carry_ref[g, rows] = carry + suffix[:, 0:1] + lsn[:, 0:1]

    last = (i * tq + tq - 1) // tk
    block(last, True)

    def body(n, c):
        block(last - 1 - n, False)
        return c

    lax.fori_loop(0, last, body, 0)
    pieces = [acc_ref[g, k * tq:(k + 1) * tq, :] for g in range(SB_KVH) for k in range(SB_G)]
    o_ref[...] = jnp.concatenate(pieces, axis=1).astype(bf16)


def sb_attn_prompt(zt, sb_t, tq, tk):
    nseq, _, t = sb_t.shape
    nq = t // tq
    r = SB_G * tq
    assert tk % tq == 0 and t % tk == 0
    return pl.pallas_call(
        functools.partial(_sb_attn_kernel, tq=tq, tk=tk),
        grid=(nseq, nq),
        in_specs=[pl.BlockSpec((tq, H * HD), lambda b, i: (b * nq + i, ZT_SQ // (H * HD))),
                  pl.BlockSpec((None, 2 * SB_KVH * HD, t), lambda b, i: (b, 0, 0))],
        out_specs=pl.BlockSpec((tq, H * HD), lambda b, i: (b * nq + i, 0)),
        out_shape=jax.ShapeDtypeStruct((nseq * t, H * HD), bf16),
        scratch_shapes=[pltpu.VMEM((SB_KVH, r, 1), f32), pltpu.VMEM((SB_KVH, r, HD), f32)],
        compiler_params=_cparams("parallel", "parallel"),
        name="sb_attn",
    )(zt, sb_t)


def _select_blocks(score, blk, n_pick):
    sel = jnp.zeros_like(score)
    sc = score
    for _ in range(n_pick):
        mx = jnp.max(sc, axis=-1, keepdims=True)
        idx = jnp.min(jnp.where(sc == mx, blk, 1e9), axis=-1, keepdims=True)
        pick = blk == idx
        sel = jnp.where(pick & (mx >= 0.0), 1.0, sel)
        sc = jnp.where(pick, -3e38, sc)
    return sel


def _alibi_key_rows(kt, kpos, q0):
    n = kt.shape[1]
    rel = kpos - q0
    rowi = lax.broadcasted_iota(i32, (HD, n), 0)
    ext = jnp.where(rowi == 0, (rel >> 6).astype(f32), jnp.where(rowi == 1, (rel & 63).astype(f32), 0.0))
    return jnp.concatenate([kt, ext.astype(bf16)], axis=0)


def _ones_value_rows(vt):
    rowi = lax.broadcasted_iota(i32, vt.shape, 0)
    return jnp.concatenate([vt, jnp.where(rowi == 0, 1.0, 0.0).astype(bf16)], axis=0)


NSA_HEADS_PER_PASS = 4


def _nsa_attn_kernel(q_ref, ng_ref, kv_ref, win_ref, pk_ref, pv_ref, o_ref, kc_ref, vc_ref, m_ref, acc_ref,
                     *, t, tq, tk):
    i = pl.program_id(1)
    n_cmp, n_sel = t // CMP_BLOCK, t // SEL_BLOCK
    r = H * tq
    hs = NSA_HEADS_PER_PASS
    rs = hs * tq
    q0 = i * tq

    @pl.when(i == 0)
    def _():
        kc_ref[...] = _dot(kv_ref[0:HD, :].astype(bf16), pk_ref[...])
        vc_ref[...] = _dot(kv_ref[HD:2 * HD, :].astype(bf16), pv_ref[...])

    slope = _slopes3()
    qs = _stack_heads(q_ref[...] * NSA_SCALE, H).astype(bf16)
    lane = lax.broadcasted_iota(i32, (1, 1, HD), 2)
    q_ext = jnp.where(lane == 0, slope * SEL_BLOCK, jnp.where(lane == 1, slope, 0.0))
    q_aug = jnp.concatenate([qs, jnp.broadcast_to(q_ext, (H, tq, HD)).reshape(r, HD).astype(bf16)], axis=1)
    qpos = q0 + lax.broadcasted_iota(i32, (1, tq, 1), 1)
    qf = qpos.astype(f32)

    cend = (lax.broadcasted_iota(i32, (1, 1, n_cmp), 2) + 1) * CMP_BLOCK - 1
    s = _dot(qs, kc_ref[...].astype(bf16)).reshape(H, tq, n_cmp) - slope * (qf - cend.astype(f32))
    p_c = _masked_softmax(s, cend <= qpos)
    o_c = _dot_nt(p_c.reshape(r, n_cmp).astype(bf16), vc_ref[...].astype(bf16)).reshape(H, tq, HD)

    imp_c = jnp.sum(p_c, axis=0)
    pair = (lax.broadcasted_iota(i32, (n_cmp, n_sel), 0) // (SEL_BLOCK // CMP_BLOCK)
            == lax.broadcasted_iota(i32, (n_cmp, n_sel), 1)).astype(bf16)
    imp = _split_dot(imp_c, pair, 3)
    blk_i = lax.broadcasted_iota(i32, (1, n_sel), 1)
    qp2 = q0 + lax.broadcasted_iota(i32, (tq, 1), 0)
    cur = qp2 // SEL_BLOCK
    forced = (blk_i == 0) | (blk_i == cur) | (blk_i == cur - 1)
    valid = blk_i * SEL_BLOCK <= qp2
    score = jnp.where(valid, imp + jnp.where(forced, FORCE_BONUS, 0.0), -1.0)
    sel = _select_blocks(score, blk_i.astype(f32), min(SEL_TOPN, n_sel)).astype(bf16)

    nw = WINDOW + tq
    start = pl.multiple_of(jnp.maximum(q0 - WINDOW, 0), PAGE)
    kpos_w = start + lax.broadcasted_iota(i32, (1, nw), 1)
    kt_w = _alibi_key_rows(win_ref[0:HD, pl.ds(start, nw)].astype(bf16), kpos_w, q0)
    vt_w = _ones_value_rows(win_ref[HD:2 * HD, pl.ds(start, nw)].astype(bf16))
    dist = qp2 - kpos_w
    mask_w = ((dist >= 0) & (dist <= WINDOW))[None]
    o_w = []
    for sp in range(H // hs):
        sm = jnp.where(mask_w, _dot(q_aug[sp * rs:(sp + 1) * rs], kt_w).reshape(hs, tq, nw), NEG).reshape(rs, nw)
        p = jnp.exp(sm - _row_max(sm))
        res = _dot_nt(p.astype(bf16), vt_w)
        o_w.append((res[:, 0:HD] / res[:, HD:HD + 1]).reshape(hs, tq, HD))
    o_w = jnp.concatenate(o_w, axis=0)

    m_ref[...] = jnp.full_like(m_ref, NEG)
    acc_ref[...] = jnp.zeros_like(acc_ref)
    for c in range(t // tk):
        @pl.when(c * tk <= q0 + tq - 1)
        def _(c=c):
            kpos2 = c * tk + lax.broadcasted_iota(i32, (1, tk), 1)
            kt = _alibi_key_rows(kv_ref[2 * HD:3 * HD, c * tk:(c + 1) * tk].astype(bf16), kpos2, q0)
            vt = _ones_value_rows(kv_ref[3 * HD:4 * HD, c * tk:(c + 1) * tk].astype(bf16))
            expand = ((c * tk + lax.broadcasted_iota(i32, (n_sel, tk), 1)) // SEL_BLOCK
                      == lax.broadcasted_iota(i32, (n_sel, tk), 0)).astype(bf16)
            mask = ((_dot(sel, expand) > 0.5) & (kpos2 <= qp2))[None]
            for sp in range(H // hs):
                rows = slice(sp * rs, (sp + 1) * rs)
                sm = jnp.where(mask, _dot(q_aug[rows], kt).reshape(hs, tq, tk), NEG).reshape(rs, tk)
                m_prev = m_ref[rows]
                m_new = jnp.maximum(m_prev, _row_max(sm))
                p = jnp.exp(sm - m_new)
                acc_ref[rows] = jnp.exp(m_prev - m_new) * acc_ref[rows] + _dot_nt(p.astype(bf16), vt)
                m_ref[rows] = m_new
    acc = acc_ref[...]
    l = acc[:, HD:HD + 1]
    o_s = (acc[:, 0:HD] / jnp.where(l > 0.0, l, 1.0)).reshape(H, tq, HD)

    gates = _sigmoid(ng_ref[...])
    col = lambda b: jnp.stack([gates[:, b * H + h:b * H + h + 1] for h in range(H)], axis=0)
    o = col(0) * o_c + col(1) * o_s + col(2) * o_w
    o_ref[...] = jnp.concatenate([o[h] for h in range(H)], axis=1).astype(bf16)


def nsa_attn_prompt(zt, nsa_t, win_t, pool_k, pool_v, tq, tk):
    nseq, _, t = nsa_t.shape
    nq = t // tq
    r = H * tq
    n_cmp = t // CMP_BLOCK
    assert tq % SEL_BLOCK == 0 and t % tk == 0
    const = lambda shape: pl.BlockSpec(shape, lambda b, i: (0,) * len(shape))
    return pl.pallas_call(
        functools.partial(_nsa_attn_kernel, t=t, tq=tq, tk=tk),
        grid=(nseq, nq),
        in_specs=[pl.BlockSpec((tq, H * HD), lambda b, i: (b * nq + i, ZT_NQ // (H * HD))),
                  pl.BlockSpec((tq, PAGE), lambda b, i: (b * nq + i, ZT_NG // PAGE)),
                  pl.BlockSpec((None, 4 * HD, t), lambda b, i: (b, 0, 0)),
                  pl.BlockSpec((None, 2 * HD, t), lambda b, i: (b, 0, 0)),
                  const((t, n_cmp)), const((t, n_cmp))],
        out_specs=pl.BlockSpec((tq, H * HD), lambda b, i: (b * nq + i, 0)),
        out_shape=jax.ShapeDtypeStruct((nseq * t, H * HD), bf16),
        scratch_shapes=[pltpu.VMEM((HD, n_cmp), f32), pltpu.VMEM((HD, n_cmp), f32),
                        pltpu.VMEM((r, 1), f32), pltpu.VMEM((r, 2 * HD), f32)],
        compiler_params=_cparams("parallel", "arbitrary"),
        name="nsa_attn",
    )(zt, zt, nsa_t, win_t, pool_k, pool_v)


def _merge_kernel(olat_ref, osb_ref, onsa_ref, gate_ref, x_ref, wuv_ref, pm_ref, ps_ref, pn_ref, wo_ref, o_ref):
    o_mla = jnp.concatenate([_dot_nt(olat_ref[h], wuv_ref[h]).astype(bf16) for h in range(H)], axis=1)
    gate = _sigmoid(gate_ref[...])
    m = (gate[:, 0:D_MODEL] * _dot(o_mla, pm_ref[...])
         + gate[:, D_MODEL:2 * D_MODEL] * _dot(osb_ref[...], ps_ref[...])
         + gate[:, 2 * D_MODEL:] * _dot(onsa_ref[...], pn_ref[...]))
    o_ref[...] = x_ref[...] + _dot(m.astype(bf16), wo_ref[...])


def merge(olat, osb, onsa, zt, x, wuv, pm, ps, pn, wo, tm):
    n = x.shape[0]
    const = lambda shape: pl.BlockSpec(shape, lambda i: (0,) * len(shape))
    row = lambda w: pl.BlockSpec((tm, w), lambda i: (i, 0))
    return pl.pallas_call(
        _merge_kernel,
        grid=(n // tm,),
        in_specs=[pl.BlockSpec((H, tm, KV_LORA), lambda i: (0, i, 0)), row(H * HD), row(H * HD), row(3 * D_MODEL),
                  row(D_MODEL), const((H, VDIM, KV_LORA)), const((H * VDIM, D_MODEL)), const((H * HD, D_MODEL)),
                  const((H * HD, D_MODEL)), const((D_MODEL, D_MODEL))],
        out_specs=row(D_MODEL),
        out_shape=jax.ShapeDtypeStruct((n, D_MODEL), f32),
        compiler_params=_cparams("parallel"),
        name="merge",
    )(olat, osb, onsa, zt, x, wuv, pm, ps, pn, wo)


FF_CHUNK = 256


def _gelu(x):
    return 0.5 * x * (1.0 + lax.erf(x * np.float32(np.sqrt(0.5))))


def _ffn_kernel(x_ref, g_ref, wup_ref, wval_ref, cw_ref, cb_ref, wdn_ref, gfin_ref, *rest, seq_mode, final_norm, nt):
    tm = x_ref.shape[0]
    if seq_mode:
        prev_ref, o_ref, st_ref, ubuf_ref, carry_ref = rest
    else:
        prev0_ref, prev1_ref, o_ref, st_ref = rest
    x = x_ref[...]
    hf = _rms_rows(x, g_ref[...]).astype(bf16)
    if seq_mode:
        @pl.when(pl.program_id(0) % nt == 0)
        def _():
            carry_ref[...] = prev_ref[...]
    acc = jnp.zeros((tm, D_MODEL), f32)
    for c in range(D_FF // FF_CHUNK):
        cols = slice(c * FF_CHUNK, (c + 1) * FF_CHUNK)
        u = _dot(hf, wup_ref[:, cols])
        v = _dot(hf, wval_ref[:, cols])
        if seq_mode:
            ubuf_ref[6:8, :] = carry_ref[:, cols]
            ubuf_ref[8:8 + tm, :] = u
            u2 = ubuf_ref[6:6 + tm, :]
            u1 = ubuf_ref[7:7 + tm, :]
            carry_ref[:, cols] = u[tm - 2:tm, :]
        else:
            u2 = prev0_ref[:, cols]
            u1 = prev1_ref[:, cols]
            st_ref[:, cols] = u
        uc = cb_ref[:, cols] + cw_ref[0:1, cols] * u2 + cw_ref[1:2, cols] * u1 + cw_ref[2:3, cols] * u
        acc = acc + _dot((_gelu(uc) * v).astype(bf16), wdn_ref[cols, :])
    if seq_mode:
        st_ref[...] = carry_ref[...]
    y = x + acc
    if final_norm:
        y = _rms_rows(y, gfin_ref[...])
    o_ref[...] = y


def ffn(x, g, wup, wval, cw, cb, wdn, gfin, prev, nseq, tm, final_norm):
    n = x.shape[0]
    seq_mode = not isinstance(prev, tuple)
    nt = (n // nseq) // tm if seq_mode else 1
    const = lambda shape: pl.BlockSpec(shape, lambda i: (0,) * len(shape))
    row = lambda w: pl.BlockSpec((tm, w), lambda i: (i, 0))
    big = lambda shape: pl.BlockSpec(shape, lambda i: (0,) * len(shape), pipeline_mode=pl.Buffered(1))
    in_specs = [row(D_MODEL), const((1, D_MODEL)), big((D_MODEL, D_FF)), big((D_MODEL, D_FF)), const((3, D_FF)),
                const((1, D_FF)), big((D_FF, D_MODEL)), const((1, D_MODEL))]
    if seq_mode:
        seq = pl.BlockSpec((None, 2, D_FF), lambda i: (i // nt, 0, 0))
        in_specs += [seq]
        out_specs = [row(D_MODEL), seq]
        out_shape = [jax.ShapeDtypeStruct((n, D_MODEL), f32), jax.ShapeDtypeStruct((nseq, 2, D_FF), f32)]
        scratch = [pltpu.VMEM((tm + 8, FF_CHUNK), f32), pltpu.VMEM((2, D_FF), f32)]
        args = (prev,)
    else:
        in_specs += [row(D_FF), row(D_FF)]
        out_specs = [row(D_MODEL), row(D_FF)]
        out_shape = [jax.ShapeDtypeStruct((n, D_MODEL), f32), jax.ShapeDtypeStruct((n, D_FF), f32)]
        scratch = []
        args = prev
    return pl.pallas_call(
        functools.partial(_ffn_kernel, seq_mode=seq_mode, final_norm=final_norm, nt=nt),
        grid=(n // tm,),
        in_specs=in_specs, out_specs=out_specs, out_shape=out_shape, scratch_shapes=scratch,
        compiler_params=_cparams("arbitrary"),
        name="ffn",
    )(x, g, wup, wval, cw, cb, wdn, gfin, *args)


def _page_specs(view_block, layer, npages, pages_per_step, order):
    def spec(p):
        def index(b, c, pt_ref):
            return (layer, pt_ref[b * npages + order(c) * pages_per_step + p]) + (0,) * (len(view_block) - 2)
        return pl.BlockSpec(view_block, index)
    return [spec(p) for p in range(pages_per_step)]


DEC_GROUPS = 1


def _pages_per_step(npages, want):
    npg = min(want, npages)
    assert npages % npg == 0
    return npg


def _mla_dec_kernel(pt_ref, q_ref, new_ref, *refs, npg, nchunk):
    pages, (o_ref, m_ref, l_ref, acc_ref) = refs[:npg], refs[npg:]
    c = pl.program_id(1)

    @pl.when(c == 0)
    def _():
        m_ref[...] = jnp.full_like(m_ref, NEG)
        l_ref[...] = jnp.zeros_like(l_ref)
        acc_ref[...] = jnp.zeros_like(acc_ref)

    q = q_ref[0]
    ngrp = min(DEC_GROUPS, npg)
    per = npg // ngrp
    parts = []
    for gi in range(ngrp):
        kt = jnp.concatenate([pg[...].astype(bf16) for pg in pages[gi * per:(gi + 1) * per]], axis=1)
        s = _dot(q, kt)
        mg = _row_max(s)
        p = jnp.exp(s - mg)
        parts.append((mg, jnp.sum(p, axis=-1, keepdims=True), _dot_nt(p.astype(bf16), kt[0:KV_LORA])))
    m_prev = m_ref[...]
    m_new = functools.reduce(jnp.maximum, [m_prev] + [part[0] for part in parts])
    alpha = jnp.exp(m_prev - m_new)
    l = alpha * l_ref[...]
    acc = alpha * acc_ref[...]
    for mg, lg, ag in parts:
        wg = jnp.exp(mg - m_new)
        l = l + wg * lg
        acc = acc + wg * ag
    m_ref[...] = m_new
    l_ref[...] = l
    acc_ref[...] = acc

    @pl.when(c == nchunk - 1)
    def _():
        knew = new_ref[0].astype(bf16).astype(f32)
        s_new = jnp.sum(q.astype(f32) * knew, axis=-1, keepdims=True)
        m_fin = jnp.maximum(m_new, s_new)
        p_new = jnp.exp(s_new - m_fin)
        a2 = jnp.exp(m_new - m_fin)
        l_fin = a2 * l + p_new
        acc_fin = a2 * acc + p_new.astype(bf16).astype(f32) * knew[:, 0:KV_LORA]
        o_ref[0] = acc_fin / l_fin


def mla_decode(cache_v, layer, pt, q, new_rows, want):
    bd = q.shape[0]
    npages = pt.shape[0] // bd
    npg = _pages_per_step(npages, want)
    nchunk = npages // npg
    specs = _page_specs((None, None, LAT, PAGE), layer, npages, npg, lambda c: c)
    return pl.pallas_call(
        functools.partial(_mla_dec_kernel, npg=npg, nchunk=nchunk),
        grid_spec=pltpu.PrefetchScalarGridSpec(
            num_scalar_prefetch=1, grid=(bd, nchunk),
            in_specs=[pl.BlockSpec((1, H, LAT), lambda b, c, p: (b, 0, 0)),
                      pl.BlockSpec((1, 1, LAT), lambda b, c, p: (b, 0, 0))] + specs,
            out_specs=pl.BlockSpec((1, H, KV_LORA), lambda b, c, p: (b, 0, 0)),
            scratch_shapes=[pltpu.VMEM((H, 1), f32), pltpu.VMEM((H, 1), f32), pltpu.VMEM((H, KV_LORA), f32)]),
        out_shape=jax.ShapeDtypeStruct((bd, H, KV_LORA), f32),
        compiler_params=_cparams("parallel", "arbitrary"),
        name="mla_dec",
    )(pt, q, new_rows, *([cache_v] * npg))


def _sb_dec_kernel(pt_ref, q_ref, *refs, npg, nchunk):
    pages, (o_ref, carry_ref, acc_ref) = refs[:npg], refs[npg:]
    c = pl.program_id(1)

    @pl.when(c == 0)
    def _():
        carry_ref[...] = jnp.zeros_like(carry_ref)
        acc_ref[...] = jnp.zeros_like(acc_ref)

    q = q_ref[0] * SB_SCALE
    rowq = lax.broadcasted_iota(i32, q.shape, 0)
    q_g = [jnp.where(rowq // SB_G == g, q, 0.0).astype(bf16) for g in range(SB_KVH)]
    tri = (lax.broadcasted_iota(i32, (PAGE, PAGE), 0) > lax.broadcasted_iota(i32, (PAGE, PAGE), 1)).astype(bf16)
    ngrp = min(DEC_GROUPS, npg)
    per = npg // ngrp
    stats = []
    for gi in range(ngrp):
        grp = pages[gi * per:(gi + 1) * per]
        z = sum(_dot(q_g[g], jnp.concatenate([pg[0, g].astype(bf16) for pg in grp], axis=1)) for g in range(SB_KVH))
        zp = jnp.concatenate([z[:, p * PAGE:(p + 1) * PAGE] for p in range(per)], axis=0)
        lsp, lsn = _log_sigmoid_pair(zp)
        suffix = _split_dot(lsn, tri, 2)
        stats.append((lsp + suffix, suffix[:, 0:1] + lsn[:, 0:1]))
    carry = carry_ref[...]
    carries = [None] * npg
    for p in reversed(range(npg)):
        carries[p] = carry
        carry = carry + stats[p // per][1][(p % per) * H:(p % per + 1) * H]
    carry_ref[...] = carry
    acc = acc_ref[...]
    for gi in range(ngrp):
        grp = pages[gi * per:(gi + 1) * per]
        a = jnp.exp(stats[gi][0] + jnp.concatenate(carries[gi * per:(gi + 1) * per], axis=0)).astype(bf16)
        aw = jnp.concatenate([a[p * H:(p + 1) * H] for p in range(per)], axis=1)
        rowa = lax.broadcasted_iota(i32, aw.shape, 0)
        for g in range(SB_KVH):
            vt = jnp.concatenate([pg[1, g].astype(bf16) for pg in grp], axis=1)
            acc = acc + _dot_nt(jnp.where(rowa // SB_G == g, aw, jnp.zeros_like(aw)), vt)
    acc_ref[...] = acc

    @pl.when(c == nchunk - 1)
    def _():
        o_ref[0] = acc


def sb_decode(cache_v, layer, pt, q, want):
    bd = q.shape[0]
    npages = pt.shape[0] // bd
    npg = _pages_per_step(npages, want)
    nchunk = npages // npg
    specs = _page_specs((None, None, 2, SB_KVH, HD, PAGE), layer, npages, npg, lambda c: nchunk - 1 - c)
    return pl.pallas_call(
        functools.partial(_sb_dec_kernel, npg=npg, nchunk=nchunk),
        grid_spec=pltpu.PrefetchScalarGridSpec(
            num_scalar_prefetch=1, grid=(bd, nchunk),
            in_specs=[pl.BlockSpec((1, H, HD), lambda b, c, p: (b, 0, 0))] + specs,
            out_specs=pl.BlockSpec((1, H, HD), lambda b, c, p: (b, 0, 0)),
            scratch_shapes=[pltpu.VMEM((H, 1), f32), pltpu.VMEM((H, HD), f32)]),
        out_shape=jax.ShapeDtypeStruct((bd, H, HD), f32),
        compiler_params=_cparams("parallel", "arbitrary"),
        name="sb_dec",
    )(pt, q, *([cache_v] * npg))


def _nsa_cmp_dec_kernel(pt_ref, q_ref, pk_ref, pv_ref, *refs, npg, nchunk, n_cmp):
    pages, (oc_ref, idx_ref, kc_ref, vc_ref) = refs[:npg], refs[npg:]
    c = pl.program_id(1)
    w = npg * PAGE // CMP_BLOCK
    ck = jnp.concatenate([pg[0].astype(bf16) for pg in pages], axis=1)
    cv = jnp.concatenate([pg[1].astype(bf16) for pg in pages], axis=1)
    kcb = _dot(ck, pk_ref[...])
    vcb = _dot(cv, pv_ref[...])
    for k in range(nchunk):
        @pl.when(c == k)
        def _(k=k):
            kc_ref[:, k * w:(k + 1) * w] = kcb
            vc_ref[:, k * w:(k + 1) * w] = vcb

    @pl.when(c == nchunk - 1)
    def _():
        q = (q_ref[0] * NSA_SCALE).astype(bf16)
        n_sel = n_cmp // 2 + 1
        qposf = jnp.float32(n_cmp * CMP_BLOCK)
        cend = ((lax.broadcasted_iota(i32, (1, n_cmp), 1) + 1) * CMP_BLOCK - 1).astype(f32)
        slope = _slopes3().reshape(H, 1)
        s = _dot(q, kc_ref[...].astype(bf16)) - slope * (qposf - cend)
        m = jnp.max(s, axis=-1, keepdims=True)
        p = jnp.exp(s - m)
        p = p / jnp.sum(p, axis=-1, keepdims=True)
        oc_ref[0] = _dot_nt(p.astype(bf16), vc_ref[...].astype(bf16))
        imp_c = jnp.sum(p, axis=0, keepdims=True)
        nl = idx_ref.shape[-1]
        pair = (lax.broadcasted_iota(i32, (n_cmp, nl), 0) // 2 == lax.broadcasted_iota(i32, (n_cmp, nl), 1)).astype(bf16)
        imp = _split_dot(jnp.broadcast_to(imp_c, (H, n_cmp)), pair, 3)[0:1]
        blk = lax.broadcasted_iota(i32, (1, nl), 1)
        cur = n_sel - 1
        forced = (blk == 0) | (blk == cur) | (blk == cur - 1)
        score = jnp.where(blk < n_sel, imp + jnp.where(forced, FORCE_BONUS, 0.0), -1.0)
        blkf = blk.astype(f32)
        out = jnp.full((1, nl), -1.0, f32)
        sc = score
        for k in range(min(SEL_TOPN, n_sel)):
            mx = jnp.max(sc, axis=-1, keepdims=True)
            idx = jnp.min(jnp.where(sc == mx, blkf, 1e9), axis=-1, keepdims=True)
            out = jnp.where((blk == k) & (mx >= 0.0), idx, out)
            sc = jnp.where(blkf == idx, -3e38, sc)
        idx_ref[0] = out.astype(i32)


def nsa_cmp_decode(cache_v, layer, pt, q, pool_k, pool_v, npg):
    bd = q.shape[0]
    npages = pt.shape[0] // bd
    nchunk = npages // npg
    n_cmp = npages * PAGE // CMP_BLOCK
    w = npg * PAGE // CMP_BLOCK
    nl = max(PAGE, -(-(n_cmp // 2 + 1) // PAGE) * PAGE)
    specs = _page_specs((None, None, 2, HD, PAGE), layer, npages, npg, lambda c: c)
    pool_spec = pl.BlockSpec((npg * PAGE, w), lambda b, c, p: (0, 0))
    return pl.pallas_call(
        functools.partial(_nsa_cmp_dec_kernel, npg=npg, nchunk=nchunk, n_cmp=n_cmp),
        grid_spec=pltpu.PrefetchScalarGridSpec(
            num_scalar_prefetch=1, grid=(bd, nchunk),
            in_specs=[pl.BlockSpec((1, H, HD), lambda b, c, p: (b, 0, 0)), pool_spec, pool_spec] + specs,
            out_specs=[pl.BlockSpec((1, H, HD), lambda b, c, p: (b, 0, 0)),
                       pl.BlockSpec((1, 1, nl), lambda b, c, p: (b, 0, 0))],
            scratch_shapes=[pltpu.VMEM((HD, n_cmp), f32), pltpu.VMEM((HD, n_cmp), f32)]),
        out_shape=[jax.ShapeDtypeStruct((bd, H, HD), f32), jax.ShapeDtypeStruct((bd, 1, nl), i32)],
        compiler_params=_cparams("parallel", "arbitrary"),
        name="nsa_cmp_dec",
    )(pt, q, pool_k, pool_v, *([cache_v] * npg))


def _nsa_sel_dec_kernel(phys_ref, blk_ref, q_ref, new_ref, oc_ref, ng_ref, win_ref, *refs, nsl, past):
    pages, (o_ref,) = refs[:nsl], refs[nsl:]
    b = pl.program_id(0)
    qf32 = q_ref[0] * NSA_SCALE
    q = qf32.astype(bf16)
    qf32 = q.astype(f32)
    slope = _slopes3().reshape(H, 1)
    new = new_ref[0].astype(bf16).astype(f32)
    lane = lax.broadcasted_iota(i32, (1, PAGE), 1)

    ss, masks = [], []
    for j in range(nsl):
        blk = blk_ref[b * nsl + j]
        tok = (blk // 2) * PAGE + lane
        valid = (blk >= 0) & (blk * SEL_BLOCK < past) & (lane // SEL_BLOCK == blk % 2)
        s = _dot(q, pages[j][0].astype(bf16)) - slope * (past - tok).astype(f32)
        ss.append(jnp.where(valid, s, NEG))
        masks.append(valid)
    s_new = jnp.sum(qf32 * new[2:3], axis=-1, keepdims=True)
    m = jnp.maximum(functools.reduce(jnp.maximum, [jnp.max(s, axis=-1, keepdims=True) for s in ss]), s_new)
    p_new = jnp.exp(s_new - m)
    l = p_new
    acc = p_new.astype(bf16).astype(f32) * new[3:4]
    for j in range(nsl):
        p = jnp.where(masks[j], jnp.exp(ss[j] - m), 0.0)
        l = l + jnp.sum(p, axis=-1, keepdims=True)
        acc = acc + _dot_nt(p.astype(bf16), pages[j][1].astype(bf16))
    o_s = acc / l

    nwin = win_ref.shape[-1]
    dist = (nwin - lax.broadcasted_iota(i32, (1, nwin), 1)).astype(f32)
    s = _dot(q, win_ref[0].astype(bf16)) - slope * dist
    s_new = jnp.sum(qf32 * new[4:5], axis=-1, keepdims=True)
    m = jnp.maximum(jnp.max(s, axis=-1, keepdims=True), s_new)
    p = jnp.exp(s - m)
    p_new = jnp.exp(s_new - m)
    l = jnp.sum(p, axis=-1, keepdims=True) + p_new
    o_w = (_dot_nt(p.astype(bf16), win_ref[1].astype(bf16)) + p_new.astype(bf16).astype(f32) * new[5:6]) / l

    gates = _sigmoid(ng_ref[0])
    o_ref[0] = gates[0] * oc_ref[0] + gates[1] * o_s + gates[2] * o_w


def nsa_sel_decode(cache_v, win_v, layer, phys, blks, q, new6, o_c, ng3, past):
    bd = q.shape[0]
    nsl = phys.shape[0] // bd
    nwin = win_v.shape[-1]

    def page_spec(j):
        return pl.BlockSpec((None, None, 2, HD, PAGE), lambda b, ph, bl: (layer, ph[b * nsl + j], 1, 0, 0))

    per_seq = lambda shape: pl.BlockSpec((1,) + shape, lambda b, ph, bl: (b,) + (0,) * len(shape))
    return pl.pallas_call(
        functools.partial(_nsa_sel_dec_kernel, nsl=nsl, past=past),
        grid_spec=pltpu.PrefetchScalarGridSpec(
            num_scalar_prefetch=2, grid=(bd,),
            in_specs=[per_seq((H, HD)), per_seq((6, HD)), per_seq((H, HD)), per_seq((3, H, 1)),
                      pl.BlockSpec((None, None, 2, HD, nwin), lambda b, ph, bl: (layer, b, 0, 0, 0))]
                     + [page_spec(j) for j in range(nsl)],
            out_specs=per_seq((H, HD))),
        out_shape=jax.ShapeDtypeStruct((bd, H, HD), f32),
        compiler_params=_cparams("parallel"),
        name="nsa_sel_dec",
    )(phys, blks, q, new6, o_c, ng3, win_v, *([cache_v] * nsl))


def _pool_matrix(w, ntok):
    tok = np.arange(ntok)
    onehot = jnp.asarray(tok[:, None] // CMP_BLOCK == np.arange(ntok // CMP_BLOCK)[None, :], f32)
    return (onehot[None] * jnp.tile(w, (1, ntok // CMP_BLOCK))[:, :, None]).astype(bf16)


def _prep_params(w_in, w_uq, w_uk, w_uv, pool_k, pool_v, p_mla, p_sb, p_nsa, w_o, w_up, w_val, w_down, t_prompt, n_dec):
    depth = w_in.shape[0]
    off = np.concatenate([[0], np.cumsum(IN_WIDTHS)])
    wt = jnp.swapaxes(w_in, 1, 2)
    seg = lambda k: wt[:, off[k]:off[k + 1]]
    cq, ckv, kr, sq, sk, sv, nq, nkv, ng, mg = [seg(k) for k in range(10)]
    half = ROPE // 2
    kr_sw = jnp.concatenate([-kr[:, half:], kr[:, :half]], axis=1)
    zpad = jnp.zeros((depth, ZT_W - ZT_NG - 3 * H, D_MODEL), f32)
    w_tok = jnp.concatenate([mg, nq, sq, cq, ng, zpad], axis=1).astype(bf16)
    w_feat = jnp.concatenate([sk, sv, nkv, ckv, kr, kr_sw], axis=1).astype(bf16)
    uq = w_uq.reshape(depth, Q_LORA, H, NOPE + ROPE)
    uq_r = uq[..., NOPE:]
    uq_sw = jnp.concatenate([-uq_r[..., half:], uq_r[..., :half]], axis=-1)
    wuq = jnp.concatenate([uq[..., :NOPE].reshape(depth, Q_LORA, -1), uq_r.reshape(depth, Q_LORA, -1),
                           uq_sw.reshape(depth, Q_LORA, -1)], axis=-1).astype(bf16)
    wuk = jnp.transpose(w_uk, (0, 2, 3, 1)).astype(bf16)
    wuv = jnp.transpose(w_uv, (0, 2, 3, 1)).astype(bf16)
    cast = lambda a: a.astype(bf16)
    return dict(w_tok=w_tok, w_feat=w_feat, wuq=wuq, wuk=wuk, wuv=wuv,
                pool_pk=_pool_matrix(pool_k, t_prompt), pool_pv=_pool_matrix(pool_v, t_prompt),
                pool_dk=_pool_matrix(pool_k, n_dec), pool_dv=_pool_matrix(pool_v, n_dec),
                p_mla=cast(p_mla), p_sb=cast(p_sb), p_nsa=cast(p_nsa), w_o=cast(w_o), w_up=cast(w_up),
                w_val=cast(w_val), w_down=cast(w_down))


def _rope_tables(pos):
    half = ROPE // 2
    inv = ROPE_BASE ** (-jnp.arange(half, dtype=f32) / half)
    ang = pos.astype(f32)[:, None] * inv[None, :]
    cos = jnp.concatenate([jnp.cos(ang)] * 2, axis=1)
    sin = jnp.concatenate([jnp.sin(ang)] * 2, axis=1)
    return jnp.tile(cos, (1, H)), jnp.tile(sin, (1, H)), cos.T, sin.T


def kernel(x_prompt, x_sample, cache_mla, cache_sb, cache_nsa, state_nsa_win, state_ffn_conv, page_table, g_attn, w_in, g_q, w_uq, g_kv, w_uk, w_uv, nsa_pool_k, nsa_pool_v, p_mla, p_sb, p_nsa, w_o, g_ffn, w_up, w_val, conv_w, conv_b, w_down, g_final):
    depth = w_in.shape[0]
    bp, t, _ = x_prompt.shape
    bd = x_sample.shape[0]
    npages = page_table.shape[1]
    past = npages * PAGE
    nwin = state_nsa_win.shape[2]
    assert x_sample.shape[1] == 1 and t % 512 == 0 and t >= WINDOW + 128 and bd % 8 == 0
    assert nwin == WINDOW and past >= WINDOW
    npg_cmp = _pages_per_step(npages, 32)
    assert npages == npg_cmp or (npg_cmp * PAGE // CMP_BLOCK) % PAGE == 0

    w = _prep_params(w_in, w_uq, w_uk, w_uv, nsa_pool_k, nsa_pool_v, p_mla, p_sb, p_nsa, w_o, w_up, w_val, w_down,
                     t, npg_cmp * PAGE)
    cos8_p, sin8_p, cost_p, sint_p = _rope_tables(jnp.arange(t))
    cos8_d, sin8_d, cost_d, sint_d = _rope_tables(jnp.full((bd,), past))
    mla_v = jnp.swapaxes(cache_mla, 2, 3)
    sb_v = jnp.transpose(cache_sb, (0, 1, 3, 4, 5, 2))
    nsa_v = jnp.transpose(cache_nsa, (0, 1, 3, 4, 5, 2)).reshape(depth, -1, 4, HD, PAGE)
    win_v = jnp.transpose(state_nsa_win, (0, 1, 3, 4, 5, 2)).reshape(depth, bd, 2, HD, nwin)
    pt = page_table.reshape(-1)
    row2 = lambda a: a.reshape(depth, 1, -1)
    gfin = g_final.reshape(1, -1)

    hp = x_prompt.reshape(bp * t, D_MODEL)
    hs = x_sample.reshape(bd, D_MODEL)
    st_p = [[] for _ in range(5)]
    st_s = [[] for _ in range(5)]
    zero_prev = jnp.zeros((bp, 2, D_FF), f32)
    for l in range(depth):
        last = l == depth - 1
        ga, gq, gf, cb = row2(g_attn)[l], row2(g_q)[l], row2(g_ffn)[l], row2(conv_b)[l]
        gkv = g_kv[l].reshape(-1, 1)
        cw = conv_w[l]

        zt, sb_t, nsa_t, win_t, mla_t = in_proj(hp, ga, w["w_tok"][l], w["w_feat"][l], bp, 256)
        q, rows_t, kt = mla_prep(zt, mla_t, cos8_p, sin8_p, cost_p, sint_p, gq, w["wuq"][l], gkv, w["wuk"][l], 256)
        o_lat = mla_attn_prompt(q, kt, 256, 512)
        o_sb = sb_attn_prompt(zt, sb_t, 256, 256)
        o_nsa = nsa_attn_prompt(zt, nsa_t, win_t, w["pool_pk"][l], w["pool_pv"][l], 128, 512)
        hp = merge(o_lat, o_sb, o_nsa, zt, hp, w["wuv"][l], w["p_mla"][l], w["p_sb"][l], w["p_nsa"][l], w["w_o"][l], 256)
        hp, conv_p = ffn(hp, gf, w["w_up"][l], w["w_val"][l], cw, cb, w["w_down"][l], gfin, zero_prev, bp, 512, last)
        for k, a in enumerate((rows_t, sb_t, nsa_t, win_t[:, :, t - WINDOW:], conv_p)):
            st_p[k].append(a)

        zt, sb_n, nsa_n, win_n, mla_t = in_proj(hs, ga, w["w_tok"][l], w["w_feat"][l], 1, bd)
        q, rows_n, _ = mla_prep(zt, mla_t, cos8_d, sin8_d, cost_d, sint_d, gq, w["wuq"][l], gkv, w["wuk"][l], bd)
        rows_n, sb_n, nsa_n, win_n = (jnp.swapaxes(a[0], 0, 1) for a in (rows_n, sb_n, nsa_n, win_n))
        o_lat = mla_decode(mla_v, l, pt, jnp.swapaxes(q, 0, 1), rows_n.reshape(bd, 1, LAT), 32)
        o_lat = jnp.swapaxes(o_lat, 0, 1).astype(bf16)
        q_sb = zt[:, ZT_SQ:ZT_SQ + H * HD].reshape(bd, H, HD)
        o_sb = sb_decode(sb_v, l, pt, q_sb, 16).reshape(bd, H * HD).astype(bf16)
        q_nsa = zt[:, ZT_NQ:ZT_NQ + H * HD].reshape(bd, H, HD)
        o_c, idx = nsa_cmp_decode(nsa_v, l, pt, q_nsa, w["pool_dk"][l], w["pool_dv"][l], npg_cmp)
        blks = idx[:, 0, :SEL_TOPN]
        phys = jnp.take_along_axis(page_table, jnp.clip(blks // 2, 0, npages - 1), axis=1)
        new6 = jnp.concatenate([nsa_n, win_n], axis=1).reshape(bd, 6, HD)
        ng3 = zt[:, ZT_NG:ZT_NG + 3 * H].reshape(bd, 3, H, 1)
        o_nsa = nsa_sel_decode(nsa_v, win_v, l, phys.reshape(-1), blks.reshape(-1), q_nsa, new6, o_c, ng3, past)
        o_nsa = o_nsa.reshape(bd, H * HD).astype(bf16)
        hs = merge(o_lat, o_sb, o_nsa, zt, hs, w["wuv"][l], w["p_mla"][l], w["p_sb"][l], w["p_nsa"][l], w["w_o"][l], bd)
        prev = (state_ffn_conv[l, :, 0], state_ffn_conv[l, :, 1])
        hs, u_new = ffn(hs, gf, w["w_up"][l], w["w_val"][l], cw, cb, w["w_down"][l], gfin, prev, bd, bd, last)
        win_s = jnp.concatenate([win_v[l][..., 1:], win_n.reshape(bd, 2, HD, 1)], axis=-1)
        for k, a in enumerate((rows_n, sb_n, nsa_n, win_s, jnp.stack([prev[1], u_new], axis=1))):
            st_s[k].append(a)

    stack = lambda xs: jnp.stack(xs)
    fm_to_tm = lambda a, shape: jnp.moveaxis(a, -1, 2).reshape(shape)
    y_prompt = hp.reshape(bp, t, D_MODEL)
    y_sample = hs.reshape(bd, 1, D_MODEL)
    new_mla_p = fm_to_tm(stack(st_p[0]), (depth, bp, t, LAT))
    new_sb_p = fm_to_tm(stack(st_p[1]), (depth, bp, t, 2, SB_KVH, HD))
    new_nsa_p = fm_to_tm(stack(st_p[2]), (depth, bp, t, 4, 1, HD))
    new_win_p = fm_to_tm(stack(st_p[3]), (depth, bp, WINDOW, 2, 1, HD))
    new_conv_p = stack(st_p[4])
    new_mla_s = stack(st_s[0]).reshape(depth, bd, 1, LAT)
    new_sb_s = stack(st_s[1]).reshape(depth, bd, 1, 2, SB_KVH, HD)
    new_nsa_s = stack(st_s[2]).reshape(depth, bd, 1, 4, 1, HD)
    new_win_s = jnp.moveaxis(stack(st_s[3]), -1, 2).reshape(depth, bd, WINDOW, 2, 1, HD)
    new_conv_s = stack(st_s[4])
    return (y_prompt, y_sample, new_mla_p, new_sb_p, new_nsa_p, new_win_p, new_conv_p,
            new_mla_s, new_sb_s, new_nsa_s, new_win_s, new_conv_s)
```
